```python
import math
import jax
import jax.numpy as jnp
from jax import lax
import numpy as np

D_MODEL = 1024
BATCH = 32
SEQ = 2048
DEPTH = 4

F32 = jnp.float32
MIX_WIDTH = D_MODEL
HALF_MIX = MIX_WIDTH // 2
DA_HEADS = 4
DA_HEAD = HALF_MIX // (2 * DA_HEADS)
DA_ROPE = DA_HEAD // 4
ROPE_THETA = 500000.0
Q_BLOCK = 128
LRU_WIDTH = HALF_MIX
LRU_BLOCKS = 8
LRU_BLOCK = LRU_WIDTH // LRU_BLOCKS
CONV_W = 4
LRU_C = 8.0
S5_WIDTH = HALF_MIX
S5_GROUP = 16
S5_GROUPS = S5_WIDTH // S5_GROUP
S5_STATE = 64
RET_HEADS = 4
RET_QK = HALF_MIX // (2 * RET_HEADS)
RET_V = HALF_MIX // RET_HEADS
RET_THETA = 10000.0
RET_CHUNK = 128
MEM_LEN = 256
XA_HEADS = 4
XA_HEAD = D_MODEL // XA_HEADS
D_FF = ((8 * D_MODEL // 3 + 127) // 128) * 128
DN_ALPHA = (2 * DEPTH) ** 0.25
DN_BETA = (8 * DEPTH) ** -0.25
LN_EPS = 1e-5
N_EVEN = (DEPTH + 1) // 2
N_ODD = DEPTH // 2
EVEN_IN = 3 * HALF_MIX + 2 * LRU_WIDTH
ODD_IN = S5_WIDTH + 2 * RET_HEADS * RET_QK + 2 * RET_HEADS * RET_V

kernel_name = 'hybrid_diffattn_rglru_s5_retention_trunk'


def layer_norm(x, g, b):
    xf = x.astype(F32)
    mu = jnp.mean(xf, -1, keepdims=True)
    var = jnp.mean(jnp.square(xf - mu), -1, keepdims=True)
    return ((xf - mu) * lax.rsqrt(var + LN_EPS) * g.astype(F32) + b.astype(F32)).astype(x.dtype)


def rms_norm(x, g):
    xf = x.astype(F32)
    return xf * lax.rsqrt(jnp.mean(jnp.square(xf), -1, keepdims=True) + LN_EPS) * g.astype(F32)


def head_group_norm(x, g):
    xf = x.astype(F32)
    mu = jnp.mean(xf, -1, keepdims=True)
    var = jnp.mean(jnp.square(xf - mu), -1, keepdims=True)
    return (xf - mu) * lax.rsqrt(var + LN_EPS) * g.astype(F32)


def rotary(x, pos, rot_dim, theta):
    half = rot_dim // 2
    inv = theta ** (-jnp.arange(half, dtype=F32) * 2.0 / rot_dim)
    ang = pos.astype(F32)[:, None] * inv[None, :]
    shape = (1, x.shape[1]) + (1,) * (x.ndim - 3) + (half,)
    cos = jnp.cos(ang).reshape(shape)
    sin = jnp.sin(ang).reshape(shape)
    xf = x.astype(F32)
    x1, x2, xp = xf[..., :half], xf[..., half:rot_dim], xf[..., rot_dim:]
    return jnp.concatenate([x1 * cos - x2 * sin, x1 * sin + x2 * cos, xp], -1).astype(x.dtype)


def swiglu(x, w_gate, w_up, w_down):
    return (jax.nn.silu(x @ w_gate) * (x @ w_up)) @ w_down


def diff_lambda_init(layer):
    return 0.8 - 0.6 * math.exp(-0.3 * layer)


def diff_attention(q, k, v, lam, norm_g, lambda_init):
    B, S, H, _, d = q.shape
    scale = d ** -0.5
    lf = lam.astype(F32)
    lmbda = jnp.exp(jnp.sum(lf[0] * lf[1])) - jnp.exp(jnp.sum(lf[2] * lf[3])) + lambda_init
    nb = S // Q_BLOCK
    qb = q.reshape(B, nb, Q_BLOCK, H, 2, d).transpose(1, 0, 2, 3, 4, 5)
    kpos = jnp.arange(S)

    def block(args):
        qi, i = args
        s = jnp.einsum('bqhmd,bkhmd->bhmqk', qi, k).astype(F32) * scale
        qpos = i * Q_BLOCK + jnp.arange(Q_BLOCK)
        causal = kpos[None, :] <= qpos[:, None]
        p = jax.nn.softmax(jnp.where(causal, s, -jnp.inf), axis=-1)
        w = p[:, :, 0] - lmbda * p[:, :, 1]
        return jnp.einsum('bhqk,bkhe->bqhe', w.astype(v.dtype), v)

    o = lax.map(block, (qb, jnp.arange(nb)))
    o = o.transpose(1, 0, 2, 3, 4).reshape(B, S, H, 2 * d)
    o = rms_norm(o, norm_g) * (1.0 - lambda_init)
    return o.reshape(B, S, H * 2 * d).astype(q.dtype)


def causal_depthwise_conv(x, w, b):
    K, C = w.shape
    y = lax.conv_general_dilated(x, w[:, None, :].astype(x.dtype), window_strides=(1,),
                                 padding=[(K - 1, 0)], dimension_numbers=('NWC', 'WIO', 'NWC'),
                                 feature_group_count=C)
    return y + b.astype(x.dtype)


def linear_combine(left, right):
    a_l, b_l = left
    a_r, b_r = right
    return a_l * a_r, a_r * b_l + b_r


def rg_lru(x, gate_w, gate_b, lam):
    B, S, W = x.shape
    xb = x.reshape(B, S, LRU_BLOCKS, LRU_BLOCK)
    gates = jnp.einsum('bsnc,gncd->gbsnd', xb, gate_w).reshape(2, B, S, W).astype(F32)
    gates = gates + gate_b.astype(F32)[:, None, None, :]
    r = jax.nn.sigmoid(gates[0])
    i = jax.nn.sigmoid(gates[1])
    log_a = -LRU_C * r * jax.nn.softplus(-lam.astype(F32))
    a = jnp.exp(log_a)
    b = jnp.sqrt(-jnp.expm1(2.0 * log_a)) * (i * x.astype(F32))
    _, h = lax.associative_scan(linear_combine, (a, b), axis=1)
    return h.astype(x.dtype)


def complex_combine(left, right):
    ar_l, ai_l, br_l, bi_l = left
    ar_r, ai_r, br_r, bi_r = right
    return (ar_r * ar_l - ai_r * ai_l, ar_r * ai_l + ai_r * ar_l,
            ar_r * br_l - ai_r * bi_l + br_r, ar_r * bi_l + ai_r * br_l + bi_r)


def s5_ssm(u, lam_re, lam_im, log_step, b_re, b_im, c_re, c_im, d_skip, glu_w, glu_b):
    B, S, W = u.shape
    uf = u.astype(F32)
    ug = uf.reshape(B, S, S5_GROUPS, S5_GROUP)
    step = jnp.exp(log_step.astype(F32))[:, None]
    lr = jnp.minimum(lam_re.astype(F32), -1e-4)
    li = lam_im.astype(F32)
    mag = jnp.exp(lr * step)
    ang = li * step
    ab_re, ab_im = mag * jnp.cos(ang), mag * jnp.sin(ang)
    den = lr * lr + li * li
    nr, ni = ab_re - 1.0, ab_im
    f_re = (nr * lr + ni * li) / den
    f_im = (ni * lr - nr * li) / den
    bu_re = jnp.einsum('bsgc,gnc->bsgn', ug, b_re.astype(F32))
    bu_im = jnp.einsum('bsgc,gnc->bsgn', ug, b_im.astype(F32))
    x_re = f_re * bu_re - f_im * bu_im
    x_im = f_re * bu_im + f_im * bu_re
    a_re = jnp.broadcast_to(ab_re[None, None], (1, S, S5_GROUPS, S5_STATE))
    a_im = jnp.broadcast_to(ab_im[None, None], (1, S, S5_GROUPS, S5_STATE))
    _, _, h_re, h_im = lax.associative_scan(complex_combine, (a_re, a_im, x_re, x_im), axis=1)
    y = (jnp.einsum('bsgn,gcn->bsgc', h_re, c_re.astype(F32))
         - jnp.einsum('bsgn,gcn->bsgc', h_im, c_im.astype(F32)))
    y = y.reshape(B, S, W) + d_skip.astype(F32) * uf
    z = jax.nn.gelu(y)
    out = z * jax.nn.sigmoid(z @ glu_w.astype(F32) + glu_b.astype(F32))
    return out.astype(u.dtype)


def retention(q, k, v, g, norm_g, pos):
    B, S, H, dk = q.shape
    dv = v.shape[-1]
    q = rotary(q, pos, dk, RET_THETA).astype(F32)
    k = rotary(k, pos, dk, RET_THETA).astype(F32) * dk ** -0.5
    v = v.astype(F32)
    log_g = jnp.log(1.0 - jnp.exp2(-5.0 - jnp.arange(H, dtype=F32)))
    C = RET_CHUNK
    nc = S // C
    idx = jnp.arange(C, dtype=F32)
    diff = idx[:, None] - idx[None, :]
    causal = diff >= 0
    decay_in = jnp.where(causal[None], jnp.exp(log_g[:, None, None] * jnp.where(causal, diff, 0.0)[None]), 0.0)
    xi = jnp.exp(log_g[:, None] * (idx + 1.0))[None, :, :, None]
    zeta = jnp.exp(log_g[:, None] * (C - 1.0 - idx))[None, :, :, None]
    chunk_decay = jnp.exp(log_g * C)[None, :, None, None]
    qc = q.reshape(B, nc, C, H, dk).transpose(1, 0, 3, 2, 4)
    kc = k.reshape(B, nc, C, H, dk).transpose(1, 0, 3, 2, 4)
    vc = v.reshape(B, nc, C, H, dv).transpose(1, 0, 3, 2, 4)

    def step(R, inp):
        qi, ki, vi = inp
        inner = jnp.einsum('bhqd,bhkd->bhqk', qi, ki) * decay_in[None]
        o = (jnp.einsum('bhqk,bhke->bhqe', inner, vi)
             + jnp.einsum('bhqd,bhde->bhqe', qi, R) * xi)
        R = chunk_decay * R + jnp.einsum('bhkd,bhke->bhde', ki * zeta, vi)
        return R, o

    R0 = jnp.zeros((B, H, dk, dv), F32)
    _, o = lax.scan(step, R0, (qc, kc, vc))
    o = o.transpose(1, 0, 3, 2, 4).reshape(B, S, H, dv)
    o = head_group_norm(o, norm_g).reshape(B, S, H * dv)
    return (jax.nn.silu(g.astype(F32)) * o).astype(g.dtype)


def even_mixer(x, pos, w_in, w_out, lam, norm_g, conv_w, conv_b, gate_w, gate_b, lru_lam, lambda_init):
    B, S, _ = x.shape
    z = x @ w_in
    q, k, v, gate, xr = jnp.split(z, [HALF_MIX, 2 * HALF_MIX, 3 * HALF_MIX, 3 * HALF_MIX + LRU_WIDTH], axis=-1)
    q = rotary(q.reshape(B, S, DA_HEADS, 2, DA_HEAD), pos, DA_ROPE, ROPE_THETA)
    k = rotary(k.reshape(B, S, DA_HEADS, 2, DA_HEAD), pos, DA_ROPE, ROPE_THETA)
    v = v.reshape(B, S, DA_HEADS, 2 * DA_HEAD)
    a_out = diff_attention(q, k, v, lam, norm_g, lambda_init)
    h = rg_lru(causal_depthwise_conv(xr, conv_w, conv_b), gate_w, gate_b, lru_lam)
    b_out = jax.nn.gelu(gate) * h
    return jnp.concatenate([a_out, b_out], axis=-1) @ w_out


def odd_mixer(x, pos, w_in, w_out, lam_re, lam_im, log_step, b_re, b_im, c_re, c_im, d_skip, glu_w, glu_b, ret_g):
    B, S, _ = x.shape
    z = x @ w_in
    o1 = S5_WIDTH
    o2 = o1 + RET_HEADS * RET_QK
    o3 = o2 + RET_HEADS * RET_QK
    o4 = o3 + RET_HEADS * RET_V
    u, q, k, v, g = jnp.split(z, [o1, o2, o3, o4], axis=-1)
    c_out = s5_ssm(u, lam_re, lam_im, log_step, b_re, b_im, c_re, c_im, d_skip, glu_w, glu_b)
    d_out = retention(q.reshape(B, S, RET_HEADS, RET_QK), k.reshape(B, S, RET_HEADS, RET_QK),
                      v.reshape(B, S, RET_HEADS, RET_V), g, ret_g, pos)
    return jnp.concatenate([c_out, d_out], axis=-1) @ w_out


def memory_cross_attention(x, mem, w_q, w_kv, w_o):
    B, S, _ = x.shape
    M = mem.shape[1]
    q = (x @ w_q).reshape(B, S, XA_HEADS, XA_HEAD)
    kv = (mem @ w_kv).reshape(B, M, 2, XA_HEADS, XA_HEAD)
    s = jnp.einsum('bshd,bmhd->bhsm', q, kv[:, :, 0]).astype(F32) * XA_HEAD ** -0.5
    p = jax.nn.softmax(s, axis=-1)
    o = jnp.einsum('bhsm,bmhd->bshd', p.astype(x.dtype), kv[:, :, 1]).reshape(B, S, D_MODEL)
    return o @ w_o


def setup_inputs(seed: int = 0) -> dict:
    key = jax.random.key(seed)
    ks = iter(jax.random.split(key, 40))

    def nrm(shape, scale):
        return jax.random.normal(next(ks), shape, F32) * scale

    x = nrm((BATCH, SEQ, D_MODEL), 1.0)
    mem = nrm((BATCH, MEM_LEN, D_MODEL), 1.0)
    ln_g = 1.0 + nrm((DEPTH, 4, D_MODEL), 0.02)
    ln_b = nrm((DEPTH, 4, D_MODEL), 0.02)
    ffn_w_gate = nrm((DEPTH, 2, D_MODEL, D_FF), D_MODEL ** -0.5)
    ffn_w_up = nrm((DEPTH, 2, D_MODEL, D_FF), D_MODEL ** -0.5)
    ffn_w_down = nrm((DEPTH, 2, D_FF, D_MODEL), D_FF ** -0.5 * DN_BETA)
    xa_w_q = nrm((DEPTH, D_MODEL, D_MODEL), D_MODEL ** -0.5)
    xa_w_kv = nrm((DEPTH, D_MODEL, 2 * D_MODEL), D_MODEL ** -0.5)
    xa_w_o = nrm((DEPTH, D_MODEL, D_MODEL), D_MODEL ** -0.5 * DN_BETA)
    ev_w_in = nrm((N_EVEN, D_MODEL, EVEN_IN), D_MODEL ** -0.5)
    ev_w_out = nrm((N_EVEN, MIX_WIDTH, D_MODEL), MIX_WIDTH ** -0.5 * DN_BETA)
    da_lambda = nrm((N_EVEN, 4, DA_HEAD), 0.1)
    da_norm_g = 1.0 + nrm((N_EVEN, 2 * DA_HEAD), 0.02)
    lru_conv_w = nrm((N_EVEN, CONV_W, LRU_WIDTH), CONV_W ** -0.5)
    lru_conv_b = nrm((N_EVEN, LRU_WIDTH), 0.02)
    lru_gate_w = nrm((N_EVEN, 2, LRU_BLOCKS, LRU_BLOCK, LRU_BLOCK), LRU_BLOCK ** -0.5)
    lru_gate_b = nrm((N_EVEN, 2, LRU_WIDTH), 0.02)
    a_c = jax.random.uniform(next(ks), (N_EVEN, LRU_WIDTH), F32, 0.9, 0.999)
    a0 = a_c ** (1.0 / LRU_C)
    lru_lambda = jnp.log(a0) - jnp.log1p(-a0)
    od_w_in = nrm((N_ODD, D_MODEL, ODD_IN), D_MODEL ** -0.5)
    od_w_out = nrm((N_ODD, MIX_WIDTH, D_MODEL), MIX_WIDTH ** -0.5 * DN_BETA)
    s5_lam_re = -0.5 + nrm((N_ODD, S5_GROUPS, S5_STATE), 0.01)
    s5_lam_im = jnp.pi * jnp.arange(S5_STATE, dtype=F32) + nrm((N_ODD, S5_GROUPS, S5_STATE), 0.01)
    s5_log_step = jax.random.uniform(next(ks), (N_ODD, S5_GROUPS), F32, math.log(1e-3), math.log(1e-1))
    s5_b_re = nrm((N_ODD, S5_GROUPS, S5_STATE, S5_GROUP), (2 * S5_GROUP) ** -0.5)
    s5_b_im = nrm((N_ODD, S5_GROUPS, S5_STATE, S5_GROUP), (2 * S5_GROUP) ** -0.5)
    s5_c_re = nrm((N_ODD, S5_GROUPS, S5_GROUP, S5_STATE), S5_STATE ** -0.5)
    s5_c_im = nrm((N_ODD, S5_GROUPS, S5_GROUP, S5_STATE), S5_STATE ** -0.5)
    s5_d = nrm((N_ODD, S5_WIDTH), 1.0)
    s5_glu_w = nrm((N_ODD, S5_WIDTH, S5_WIDTH), S5_WIDTH ** -0.5)
    s5_glu_b = nrm((N_ODD, S5_WIDTH), 0.02)
    ret_norm_g = 1.0 + nrm((N_ODD, RET_V), 0.02)
    return {'x': x, 'mem': mem, 'ln_g': ln_g, 'ln_b': ln_b,
            'ffn_w_gate': ffn_w_gate, 'ffn_w_up': ffn_w_up, 'ffn_w_down': ffn_w_down,
            'xa_w_q': xa_w_q, 'xa_w_kv': xa_w_kv, 'xa_w_o': xa_w_o,
            'ev_w_in': ev_w_in, 'ev_w_out': ev_w_out, 'da_lambda': da_lambda, 'da_norm_g': da_norm_g,
            'lru_conv_w': lru_conv_w, 'lru_conv_b': lru_conv_b, 'lru_gate_w': lru_gate_w,
            'lru_gate_b': lru_gate_b, 'lru_lambda': lru_lambda,
            'od_w_in': od_w_in, 'od_w_out': od_w_out, 's5_lam_re': s5_lam_re, 's5_lam_im': s5_lam_im,
            's5_log_step': s5_log_step, 's5_b_re': s5_b_re, 's5_b_im': s5_b_im,
            's5_c_re': s5_c_re, 's5_c_im': s5_c_im, 's5_d': s5_d, 's5_glu_w': s5_glu_w,
            's5_glu_b': s5_glu_b, 'ret_norm_g': ret_norm_g}


def reference(x, mem, ln_g, ln_b, ffn_w_gate, ffn_w_up, ffn_w_down, xa_w_q, xa_w_kv, xa_w_o,
              ev_w_in, ev_w_out, da_lambda, da_norm_g, lru_conv_w, lru_conv_b, lru_gate_w,
              lru_gate_b, lru_lambda, od_w_in, od_w_out, s5_lam_re, s5_lam_im, s5_log_step,
              s5_b_re, s5_b_im, s5_c_re, s5_c_im, s5_d, s5_glu_w, s5_glu_b, ret_norm_g):
    S = x.shape[1]
    pos = jnp.arange(S, dtype=jnp.int32)
    for l in range(DEPTH):
        h = swiglu(x, ffn_w_gate[l, 0], ffn_w_up[l, 0], ffn_w_down[l, 0])
        x = layer_norm(DN_ALPHA * x + 0.5 * h, ln_g[l, 0], ln_b[l, 0])
        if l % 2 == 0:
            e = l // 2
            m = even_mixer(x, pos, ev_w_in[e], ev_w_out[e], da_lambda[e], da_norm_g[e],
                           lru_conv_w[e], lru_conv_b[e], lru_gate_w[e], lru_gate_b[e],
                           lru_lambda[e], diff_lambda_init(l))
        else:
            o = l // 2
            m = odd_mixer(x, pos, od_w_in[o], od_w_out[o], s5_lam_re[o], s5_lam_im[o],
                          s5_log_step[o], s5_b_re[o], s5_b_im[o], s5_c_re[o], s5_c_im[o],
                          s5_d[o], s5_glu_w[o], s5_glu_b[o], ret_norm_g[o])
        x = layer_norm(DN_ALPHA * x + m, ln_g[l, 1], ln_b[l, 1])
        c = memory_cross_attention(x, mem, xa_w_q[l], xa_w_kv[l], xa_w_o[l])
        x = layer_norm(DN_ALPHA * x + c, ln_g[l, 2], ln_b[l, 2])
        h = swiglu(x, ffn_w_gate[l, 1], ffn_w_up[l, 1], ffn_w_down[l, 1])
        x = layer_norm(DN_ALPHA * x + 0.5 * h, ln_g[l, 3], ln_b[l, 3])
    return x
```

```python
import functools
import math

import jax
import jax.numpy as jnp
from jax import lax
from jax.experimental import pallas as pl
from jax.experimental.pallas import tpu as pltpu

F32 = jnp.float32
BF16 = jnp.bfloat16

D_MODEL = 1024
DEPTH = 4
HALF_MIX = 512
DA_HEADS = 4
DA_HEAD = 64
DA_ROPE = 16
ROPE_THETA = 500000.0
LRU_BLOCKS = 8
LRU_BLOCK = 64
CONV_W = 4
LRU_C = 8.0
S5_GROUP = 16
S5_GROUPS = 32
S5_STATE = 64
RET_HEADS = 4
RET_QK = 64
RET_V = 128
RET_THETA = 10000.0
RET_CHUNK = 128
MEM_LEN = 256
XA_HEADS = 4
XA_HEAD = 256
D_FF = 2816
DN_ALPHA = (2 * DEPTH) ** 0.25
LN_EPS = 1e-5

LANES = 128
SUBLANES = 8
VMEM_LIMIT = 56 * 1024 * 1024

TM = 512
FF_CHUNK = 256
TQ = 256
SCAN_BATCH = SUBLANES
LRU_T = 256
S5_T = 128
PITCH_PAD = 4
NEG_BIG = -1e30


def _cparams(sem):
    return pltpu.CompilerParams(dimension_semantics=sem, vmem_limit_bytes=VMEM_LIMIT)


def _const_spec(shape):
    nd = len(shape)
    return pl.BlockSpec(shape, lambda *_: (0,) * nd, pipeline_mode=pl.Buffered(1))


def _layer_norm(y, g, b):
    mu = jnp.mean(y, axis=-1, keepdims=True)
    yc = y - mu
    var = jnp.mean(yc * yc, axis=-1, keepdims=True)
    return yc * lax.rsqrt(var + LN_EPS) * g + b


def _ffn_kernel(x_ref, wg_ref, wu_ref, wd_ref, g_ref, b_ref, o_ref, xb_ref, acc_ref):
    xb_ref[...] = x_ref[...].astype(BF16)
    acc_ref[...] = jnp.zeros_like(acc_ref)

    def body(c, carry):
        xb = xb_ref[...]
        g = jnp.dot(xb, wg_ref[c], preferred_element_type=F32)
        u = jnp.dot(xb, wu_ref[c], preferred_element_type=F32)
        a = (g * jax.nn.sigmoid(g) * u).astype(BF16)
        acc_ref[...] += jnp.dot(a, wd_ref[c], preferred_element_type=F32)
        return carry

    lax.fori_loop(0, wg_ref.shape[0], body, 0)
    y = DN_ALPHA * x_ref[...] + 0.5 * acc_ref[...]
    o_ref[...] = _layer_norm(y, g_ref[...], b_ref[...])


def ffn_ln(x, wg, wu, wd, g, b):
    n, d = x.shape
    nc, _, fc = wg.shape
    return pl.pallas_call(
        _ffn_kernel,
        grid=(n // TM,),
        in_specs=[
            pl.BlockSpec((TM, d), lambda i: (i, 0)),
            _const_spec((nc, d, fc)),
            _const_spec((nc, d, fc)),
            _const_spec((nc, fc, d)),
            _const_spec((1, d)),
            _const_spec((1, d)),
        ],
        out_specs=pl.BlockSpec((TM, d), lambda i: (i, 0)),
        out_shape=jax.ShapeDtypeStruct((n, d), F32),
        scratch_shapes=[pltpu.VMEM((TM, d), BF16), pltpu.VMEM((TM, d), F32)],
        compiler_params=_cparams(("parallel",)),
        name="ffn_ln",
    )(x, wg, wu, wd, g, b)


def _in_proj_kernel(x_ref, w_ref, cos_ref, sa_ref, sb_ref, *o_refs, sections, rot_shift):
    xb = x_ref[...].astype(BF16)
    for (start, width, rotate, scale), o_ref in zip(sections, o_refs):
        z = jnp.dot(xb, w_ref[:, start:start + width], preferred_element_type=F32)
        if rotate:
            cos, sa, sb = cos_ref[...], sa_ref[...], sb_ref[...]
            for j in range(width // LANES):
                zj = z[:, j * LANES:(j + 1) * LANES]
                zr = (zj * cos + pltpu.roll(zj, LANES - rot_shift, 1) * sa
                      + pltpu.roll(zj, rot_shift, 1) * sb)
                o_ref[:, j * LANES:(j + 1) * LANES] = (zr * scale).astype(o_ref.dtype)
        else:
            o_ref[...] = (z * scale).astype(o_ref.dtype)


def in_proj(x, w, tables, sections, out_dtypes, rot_shift, seq):
    n, d = x.shape
    tiles_per_seq = seq // TM
    tab_spec = pl.BlockSpec((TM, LANES), lambda i: (i % tiles_per_seq, 0))
    return pl.pallas_call(
        functools.partial(_in_proj_kernel, sections=sections, rot_shift=rot_shift),
        grid=(n // TM,),
        in_specs=[pl.BlockSpec((TM, d), lambda i: (i, 0)), _const_spec(w.shape),
                  tab_spec, tab_spec, tab_spec],
        out_specs=[pl.BlockSpec((TM, s[1]), lambda i: (i, 0)) for s in sections],
        out_shape=[jax.ShapeDtypeStruct((n, s[1]), dt) for s, dt in zip(sections, out_dtypes)],
        compiler_params=_cparams(("parallel",)),
        name="in_proj",
    )(x, w, *tables)


def _rotary_tables(seq, rot_dim, theta, group):
    half = rot_dim // 2
    inv = theta ** (-jnp.arange(half, dtype=F32) * 2.0 / rot_dim)
    ang = jnp.arange(seq, dtype=jnp.int32).astype(F32)[:, None] * inv[None, :]
    cos, sin = jnp.cos(ang), jnp.sin(ang)
    zeros = jnp.zeros((seq, group - rot_dim), F32)
    zh = jnp.zeros((seq, half), F32)
    cos_g = jnp.concatenate([cos, cos, jnp.ones((seq, group - rot_dim), F32)], -1)
    sa_g = jnp.concatenate([-sin, zh, zeros], -1)
    sb_g = jnp.concatenate([zh, sin, zeros], -1)
    reps = LANES // group
    return tuple(jnp.tile(t, (1, reps)) for t in (cos_g, sa_g, sb_g))


def _out_proj_kernel(x_ref, a_ref, b_ref, w_ref, g_ref, beta_ref, o_ref):
    half = a_ref.shape[1]
    m = jnp.dot(a_ref[...], w_ref[:half, :], preferred_element_type=F32)
    m = m + jnp.dot(b_ref[...], w_ref[half:, :], preferred_element_type=F32)
    o_ref[...] = _layer_norm(DN_ALPHA * x_ref[...] + m, g_ref[...], beta_ref[...])


def out_proj_ln(x, a, b, w, g, beta):
    n, d = x.shape
    half = a.shape[1]
    return pl.pallas_call(
        _out_proj_kernel,
        grid=(n // TM,),
        in_specs=[pl.BlockSpec((TM, d), lambda i: (i, 0)),
                  pl.BlockSpec((TM, half), lambda i: (i, 0)),
                  pl.BlockSpec((TM, half), lambda i: (i, 0)),
                  _const_spec(w.shape), _const_spec((1, d)), _const_spec((1, d))],
        out_specs=pl.BlockSpec((TM, d), lambda i: (i, 0)),
        out_shape=jax.ShapeDtypeStruct((n, d), F32),
        compiler_params=_cparams(("parallel",)),
        name="out_proj_ln",
    )(x, a, b, w, g, beta)


def _diff_attn_kernel(q_ref, k_ref, v_ref, lam_ref, g_ref, o_ref, m_ref, l_ref, acc_ref,
                      *, lambda_init):
    qi = pl.program_id(2)
    tq = q_ref.shape[0]
    lane = lax.broadcasted_iota(jnp.int32, (tq, LANES), 1)
    qf = q_ref[...].astype(F32)
    qq = jnp.concatenate([jnp.where(lane < DA_HEAD, qf, 0.0),
                          jnp.where(lane >= DA_HEAD, qf, 0.0)], axis=0).astype(BF16)
    m_ref[...] = jnp.full_like(m_ref, NEG_BIG)
    l_ref[...] = jnp.zeros_like(l_ref)
    acc_ref[...] = jnp.zeros_like(acc_ref)

    def block(j, masked):
        start = pl.multiple_of(j * tq, tq)
        kb = k_ref[pl.ds(start, tq), :]
        vb = v_ref[pl.ds(start, tq), :]
        s = lax.dot_general(qq, kb, (((1,), (1,)), ((), ())), preferred_element_type=F32)
        if masked:
            row = lax.broadcasted_iota(jnp.int32, s.shape, 0)
            col = lax.broadcasted_iota(jnp.int32, s.shape, 1)
            row = jnp.where(row >= tq, row - tq, row)
            s = jnp.where(col <= row, s, NEG_BIG)
        m_old = m_ref[...]
        m_new = jnp.maximum(m_old, jnp.max(s, axis=-1, keepdims=True))
        p = jnp.exp(s - m_new)
        alpha = jnp.exp(m_old - m_new)
        l_ref[...] = alpha * l_ref[...] + jnp.sum(p, axis=-1, keepdims=True)
        acc_ref[...] = alpha * acc_ref[...] + jnp.dot(p.astype(BF16), vb,
                                                      preferred_element_type=F32)
        m_ref[...] = m_new

    def body(j, carry):
        block(j, False)
        return carry

    lax.fori_loop(0, qi, body, 0)
    block(qi, True)

    o = acc_ref[...] / l_ref[...]
    lam = lam_ref[...]
    lmbda = (jnp.exp(jnp.sum(lam[0:1] * lam[1:2], axis=-1, keepdims=True))
             - jnp.exp(jnp.sum(lam[2:3] * lam[3:4], axis=-1, keepdims=True)) + lambda_init)
    d = o[:tq] - lmbda * o[tq:]
    rms = lax.rsqrt(jnp.mean(d * d, axis=-1, keepdims=True) + LN_EPS)
    o_ref[...] = (d * rms * g_ref[...] * (1.0 - lambda_init)).astype(o_ref.dtype)


def diff_attn(q, k, v, lam, norm_g, lambda_init):
    bsz, seq, width = q.shape
    heads = width // LANES
    kv_spec = pl.BlockSpec((None, seq, LANES), lambda b, h, i: (b, 0, h))
    return pl.pallas_call(
        functools.partial(_diff_attn_kernel, lambda_init=lambda_init),
        grid=(bsz, heads, seq // TQ),
        in_specs=[pl.BlockSpec((None, TQ, LANES), lambda b, h, i: (b, i, h)),
                  kv_spec, kv_spec,
                  _const_spec(lam.shape), _const_spec(norm_g.shape)],
        out_specs=pl.BlockSpec((None, TQ, LANES), lambda b, h, i: (b, i, h)),
        out_shape=jax.ShapeDtypeStruct((bsz, seq, width), BF16),
        scratch_shapes=[pltpu.VMEM((2 * TQ, 1), F32), pltpu.VMEM((2 * TQ, 1), F32),
                        pltpu.VMEM((2 * TQ, LANES), F32)],
        compiler_params=_cparams(("parallel", "parallel", "arbitrary")),
        name="diff_attn",
    )(q, k, v, lam, norm_g)


def _pitch(t_steps):
    return t_steps + PITCH_PAD


def _rows(t_steps):
    return SCAN_BATCH * _pitch(t_steps)


def _seq_row0(b, t_steps):
    return b * _pitch(t_steps) + PITCH_PAD


def _time_rows(t, t_steps):
    return pl.ds(PITCH_PAD + t, SCAN_BATCH, stride=_pitch(t_steps))


CONV_HEAD = SUBLANES


def _rglru_kernel(xr_ref, gate_ref, cw_ref, cb_ref, wg_ref, gb_ref, lam_ref, o_ref,
                  xpad_ref, a_ref, b_ref, h_ref, *, t_steps, row_chunks):
    ti = pl.program_id(1)
    pitch, rows = _pitch(t_steps), _rows(t_steps)
    width = xr_ref.shape[2]
    nslab = width // LANES
    hist = CONV_W - 1

    @pl.when(ti == 0)
    def _():
        xpad_ref[...] = jnp.zeros_like(xpad_ref)
        h_ref[...] = jnp.zeros_like(h_ref)

    @pl.when(ti > 0)
    def _():
        for b in range(SCAN_BATCH):
            r0 = CONV_HEAD + _seq_row0(b, t_steps)
            xpad_ref[pl.ds(r0 - hist, hist), :] = xpad_ref[pl.ds(r0 + t_steps - hist, hist), :]

    for b in range(SCAN_BATCH):
        xpad_ref[pl.ds(CONV_HEAD + _seq_row0(b, t_steps), t_steps), :] = xr_ref[b]

    softplus = jax.nn.softplus(-lam_ref[...])
    chunk = rows // row_chunks
    for c in range(row_chunks):
        r0 = c * chunk
        xc = cb_ref[...] + cw_ref[0:1, :] * xpad_ref[pl.ds(CONV_HEAD + r0 - hist, chunk), :]
        for j in range(1, CONV_W):
            xc = xc + cw_ref[j:j + 1, :] * xpad_ref[pl.ds(CONV_HEAD + r0 - hist + j, chunk), :]
        gates = jnp.dot(xc.astype(BF16), wg_ref[...], preferred_element_type=F32) + gb_ref[...]
        r = jax.nn.sigmoid(gates[:, :width])
        i = jax.nn.sigmoid(gates[:, width:])
        log_a = -LRU_C * r * softplus
        a = jnp.exp(log_a)
        bb = jnp.sqrt(-jnp.tanh(log_a) * (a * a + 1.0)) * (i * xc)
        for s in range(nslab):
            a_ref[s, pl.ds(r0, chunk), :] = a[:, s * LANES:(s + 1) * LANES]
            b_ref[s, pl.ds(r0, chunk), :] = bb[:, s * LANES:(s + 1) * LANES]

    def step(t, hs):
        new = []
        for s in range(nslab):
            rows_t = _time_rows(t, t_steps)
            h = a_ref[s, rows_t, :] * hs[s] + b_ref[s, rows_t, :]
            b_ref[s, rows_t, :] = h
            new.append(h)
        return tuple(new)

    hs = lax.fori_loop(0, t_steps, step, tuple(h_ref[s] for s in range(nslab)), unroll=8)
    for s in range(nslab):
        h_ref[s] = hs[s]

    for b in range(SCAN_BATCH):
        r0 = _seq_row0(b, t_steps)
        for s in range(nslab):
            h = b_ref[s, pl.ds(r0, t_steps), :]
            gt = gate_ref[b, :, s * LANES:(s + 1) * LANES]
            o_ref[b, :, s * LANES:(s + 1) * LANES] = (jax.nn.gelu(gt) * h).astype(o_ref.dtype)


def rglru(xr, gate, conv_w, conv_b, w_gates, gate_b, lru_lam, t_steps=None):
    bsz, seq, width = xr.shape
    t_steps = t_steps or min(LRU_T, seq)
    rows = _rows(t_steps)
    row_chunks = next(c for c in (5, 4, 3, 2, 1) if (rows // SUBLANES) % c == 0)
    blk = pl.BlockSpec((SCAN_BATCH, t_steps, width), lambda g, t: (g, t, 0))
    return pl.pallas_call(
        functools.partial(_rglru_kernel, t_steps=t_steps, row_chunks=row_chunks),
        grid=(bsz // SCAN_BATCH, seq // t_steps),
        in_specs=[blk, blk, _const_spec(conv_w.shape), _const_spec(conv_b.shape),
                  _const_spec(w_gates.shape), _const_spec(gate_b.shape),
                  _const_spec(lru_lam.shape)],
        out_specs=blk,
        out_shape=jax.ShapeDtypeStruct((bsz, seq, width), BF16),
        scratch_shapes=[pltpu.VMEM((CONV_HEAD + rows, width), F32),
                        pltpu.VMEM((width // LANES, rows, LANES), F32),
                        pltpu.VMEM((width // LANES, rows, LANES), F32),
                        pltpu.VMEM((width // LANES, SCAN_BATCH, LANES), F32)],
        compiler_params=_cparams(("parallel", "arbitrary")),
        name="rglru",
    )(xr, gate, conv_w, conv_b, w_gates, gate_b, lru_lam)


S5_PAIRS = S5_GROUPS // 2
S5_PAIRS_PER_PASS = 4
S5_PAIRS_PER_BLOCK = LANES // (2 * S5_GROUP)


def _s5_kernel(u_ref, wb_ref, are_ref, aim_ref, wc_ref, d_ref, wglu_ref, bglu_ref, o_ref,
               upad_ref, xs_ref, h_ref, y_ref, *, t_steps):
    ti = pl.program_id(1)
    rows = _rows(t_steps)
    width = u_ref.shape[2]

    @pl.when(ti == 0)
    def _():
        upad_ref[...] = jnp.zeros_like(upad_ref)
        h_ref[...] = jnp.zeros_like(h_ref)

    for b in range(SCAN_BATCH):
        upad_ref[pl.ds(_seq_row0(b, t_steps), t_steps), :] = u_ref[b]

    for p in range(S5_PAIRS):
        blk = p // S5_PAIRS_PER_BLOCK
        ub = upad_ref[:, blk * LANES:(blk + 1) * LANES].astype(BF16)
        x = jnp.dot(ub, wb_ref[p], preferred_element_type=F32)
        xs_ref[2 * p] = x[:, :LANES]
        xs_ref[2 * p + 1] = x[:, LANES:]

    for p0 in range(0, S5_PAIRS, S5_PAIRS_PER_PASS):
        pairs = range(p0, p0 + S5_PAIRS_PER_PASS)
        a_re = [jnp.broadcast_to(are_ref[p:p + 1, :], (SCAN_BATCH, LANES)) for p in pairs]
        a_im = [jnp.broadcast_to(aim_ref[p:p + 1, :], (SCAN_BATCH, LANES)) for p in pairs]

        def step(t, hs, pairs=pairs, a_re=a_re, a_im=a_im):
            new = []
            rows_t = _time_rows(t, t_steps)
            for n, p in enumerate(pairs):
                h_re, h_im = hs[2 * n], hs[2 * n + 1]
                n_re = a_re[n] * h_re - a_im[n] * h_im + xs_ref[2 * p, rows_t, :]
                n_im = a_re[n] * h_im + a_im[n] * h_re + xs_ref[2 * p + 1, rows_t, :]
                xs_ref[2 * p, rows_t, :] = n_re
                xs_ref[2 * p + 1, rows_t, :] = n_im
                new += [n_re, n_im]
            return tuple(new)

        init = tuple(h_ref[s] for p in pairs for s in (2 * p, 2 * p + 1))
        hs = lax.fori_loop(0, t_steps, step, init, unroll=4)
        for n, p in enumerate(pairs):
            h_ref[2 * p] = hs[2 * n]
            h_ref[2 * p + 1] = hs[2 * n + 1]

    pairs_per_out = S5_PAIRS // (width // LANES)
    for q in range(width // LANES):
        y = d_ref[:, q * LANES:(q + 1) * LANES] * upad_ref[:, q * LANES:(q + 1) * LANES]
        for p in range(q * pairs_per_out, (q + 1) * pairs_per_out):
            hcat = jnp.concatenate([xs_ref[2 * p], xs_ref[2 * p + 1]], axis=-1).astype(BF16)
            y = y + jnp.dot(hcat, wc_ref[p], preferred_element_type=F32)
        y_ref[:, q * LANES:(q + 1) * LANES] = jax.nn.gelu(y)

    z = y_ref[...]
    gate = jnp.dot(z.astype(BF16), wglu_ref[...], preferred_element_type=F32) + bglu_ref[...]
    y_ref[...] = z * jax.nn.sigmoid(gate)
    for b in range(SCAN_BATCH):
        o_ref[b] = y_ref[pl.ds(_seq_row0(b, t_steps), t_steps), :].astype(o_ref.dtype)


def s5(u, wb, a_re, a_im, wc, d_skip, w_glu, b_glu, t_steps=None):
    bsz, seq, width = u.shape
    t_steps = t_steps or min(S5_T, seq)
    rows = _rows(t_steps)
    blk = pl.BlockSpec((SCAN_BATCH, t_steps, width), lambda g, t: (g, t, 0))
    nslab = 2 * S5_PAIRS
    return pl.pallas_call(
        functools.partial(_s5_kernel, t_steps=t_steps),
        grid=(bsz // SCAN_BATCH, seq // t_steps),
        in_specs=[blk, _const_spec(wb.shape), _const_spec(a_re.shape), _const_spec(a_im.shape),
                  _const_spec(wc.shape), _const_spec(d_skip.shape), _const_spec(w_glu.shape),
                  _const_spec(b_glu.shape)],
        out_specs=blk,
        out_shape=jax.ShapeDtypeStruct((bsz, seq, width), BF16),
        scratch_shapes=[pltpu.VMEM((rows, width), F32),
                        pltpu.VMEM((nslab, rows, LANES), F32),
                        pltpu.VMEM((nslab, SCAN_BATCH, LANES), F32),
                        pltpu.VMEM((rows, width), F32)],
        compiler_params=_cparams(("parallel", "arbitrary")),
        name="s5",
    )(u, wb, a_re, a_im, wc, d_skip, w_glu, b_glu)


def _s5_params(lam_re, lam_im, log_step, b_re, b_im, c_re, c_im):
    step = jnp.exp(log_step.astype(F32))[:, None]
    lr = jnp.minimum(lam_re.astype(F32), -1e-4)
    li = lam_im.astype(F32)
    mag = jnp.exp(lr * step)
    ang = li * step
    ab_re, ab_im = mag * jnp.cos(ang), mag * jnp.sin(ang)
    den = lr * lr + li * li
    nr, ni = ab_re - 1.0, ab_im
    f_re = (nr * lr + ni * li) / den
    f_im = (ni * lr - nr * li) / den
    bb_re = f_re[:, :, None] * b_re - f_im[:, :, None] * b_im
    bb_im = f_re[:, :, None] * b_im + f_im[:, :, None] * b_re
    g, n, c = bb_re.shape
    eye2 = jnp.eye(2, dtype=F32)
    eye_b = jnp.eye(S5_PAIRS_PER_BLOCK, dtype=F32)

    def in_mat(bb):
        bt = bb.transpose(0, 2, 1).reshape(S5_PAIRS, 2, c, n)
        m = jnp.einsum('pgcn,gh->pgchn', bt, eye2).reshape(S5_PAIRS, 2 * c, 2 * n)
        slot = jnp.arange(S5_PAIRS) % S5_PAIRS_PER_BLOCK
        sel = eye_b[slot]
        return jnp.einsum('pkn,ps->pskn', m, sel).reshape(S5_PAIRS, LANES, 2 * n)

    wb = jnp.concatenate([in_mat(bb_re), in_mat(bb_im)], axis=-1).astype(BF16)

    pairs_per_out = S5_PAIRS // (HALF_MIX // LANES)
    eye_o = jnp.eye(pairs_per_out, dtype=F32)

    def out_mat(cc):
        ct = cc.transpose(0, 2, 1).reshape(S5_PAIRS, 2, n, c)
        m = jnp.einsum('pgnc,gh->pgnhc', ct, eye2).reshape(S5_PAIRS, 2 * n, 2 * c)
        slot = jnp.arange(S5_PAIRS) % pairs_per_out
        sel = eye_o[slot]
        return jnp.einsum('pnk,ps->pnsk', m, sel).reshape(S5_PAIRS, 2 * n, LANES)

    wc = jnp.concatenate([out_mat(c_re.astype(F32)), -out_mat(c_im.astype(F32))],
                         axis=1).astype(BF16)
    a_re = ab_re.reshape(S5_PAIRS, 2 * n)
    a_im = ab_im.reshape(S5_PAIRS, 2 * n)
    return wb, a_re, a_im, wc


def _retention_kernel(q_ref, k_ref, v_ref, g_ref, ng_ref, din_ref, xi_ref, zeta_ref, cd_ref,
                      o_ref, r_ref):
    h = pl.program_id(1)
    seq = q_ref.shape[0]
    c = RET_CHUNK
    lane = lax.broadcasted_iota(jnp.int32, (c, LANES), 1)
    mine = (lane // RET_QK) == (h % (LANES // RET_QK))
    r_ref[...] = jnp.zeros_like(r_ref)

    def body(ci, carry):
        rows = pl.ds(pl.multiple_of(ci * c, c), c)
        qm = jnp.where(mine, q_ref[rows, :].astype(F32), 0.0)
        km = jnp.where(mine, k_ref[rows, :].astype(F32), 0.0)
        qb = qm.astype(BF16)
        vb = v_ref[rows, :]
        inner = lax.dot_general(qb, km.astype(BF16), (((1,), (1,)), ((), ())),
                                preferred_element_type=F32) * din_ref[...]
        o = jnp.dot(inner.astype(BF16), vb, preferred_element_type=F32)
        o = o + jnp.dot(qb, r_ref[...].astype(BF16), preferred_element_type=F32) * xi_ref[...]
        kz = (km * zeta_ref[...]).astype(BF16)
        r_ref[...] = cd_ref[...] * r_ref[...] + lax.dot_general(
            kz, vb, (((0,), (0,)), ((), ())), preferred_element_type=F32)
        mu = jnp.mean(o, axis=-1, keepdims=True)
        oc = o - mu
        var = jnp.mean(oc * oc, axis=-1, keepdims=True)
        on = oc * lax.rsqrt(var + LN_EPS) * ng_ref[...]
        gt = g_ref[rows, :]
        o_ref[rows, :] = (gt * jax.nn.sigmoid(gt) * on).astype(o_ref.dtype)
        return carry

    lax.fori_loop(0, seq // c, body, 0)


def retention(q, k, v, g, norm_g, tables):
    bsz, seq, _ = q.shape
    width = v.shape[2]
    heads = width // LANES
    heads_per_blk = LANES // RET_QK
    qk_spec = pl.BlockSpec((None, seq, LANES), lambda b, h: (b, 0, h // heads_per_blk))
    v_spec = pl.BlockSpec((None, seq, LANES), lambda b, h: (b, 0, h))
    tab_spec = pl.BlockSpec((None, RET_CHUNK, LANES), lambda b, h: (h, 0, 0))
    cd_spec = pl.BlockSpec((None, 1, LANES), lambda b, h: (h, 0, 0))
    return pl.pallas_call(
        _retention_kernel,
        grid=(bsz, heads),
        in_specs=[qk_spec, qk_spec, v_spec, v_spec, _const_spec(norm_g.shape),
                  tab_spec, tab_spec, tab_spec, cd_spec],
        out_specs=v_spec,
        out_shape=jax.ShapeDtypeStruct((bsz, seq, width), BF16),
        scratch_shapes=[pltpu.VMEM((LANES, LANES), F32)],
        compiler_params=_cparams(("parallel", "arbitrary")),
        name="retention",
    )(q, k, v, g, norm_g, *tables)


def _retention_tables():
    c = RET_CHUNK
    log_g = jnp.log(1.0 - jnp.exp2(-5.0 - jnp.arange(RET_HEADS, dtype=F32)))
    idx = jnp.arange(c, dtype=F32)
    diff = idx[:, None] - idx[None, :]
    causal = diff >= 0
    decay_in = jnp.where(causal[None], jnp.exp(log_g[:, None, None] * jnp.where(causal, diff, 0.0)[None]), 0.0)
    xi = jnp.exp(log_g[:, None] * (idx + 1.0))
    zeta = jnp.exp(log_g[:, None] * (c - 1.0 - idx))
    chunk_decay = jnp.exp(log_g * c)
    bc = lambda t: jnp.broadcast_to(t[:, :, None], (RET_HEADS, c, LANES))
    cd = jnp.broadcast_to(chunk_decay[:, None, None], (RET_HEADS, 1, LANES))
    return decay_in, bc(xi), bc(zeta), cd


def _kv_proj_kernel(m_ref, w_ref, k_ref, v_ref):
    mb = m_ref[...].astype(BF16)
    d = k_ref.shape[1]
    k_ref[...] = jnp.dot(mb, w_ref[:, :d], preferred_element_type=F32).astype(k_ref.dtype)
    v_ref[...] = jnp.dot(mb, w_ref[:, d:], preferred_element_type=F32).astype(v_ref.dtype)


def kv_proj(mem, w):
    n, d = mem.shape
    return pl.pallas_call(
        _kv_proj_kernel,
        grid=(n // TM,),
        in_specs=[pl.BlockSpec((TM, d), lambda i: (i, 0)), _const_spec(w.shape)],
        out_specs=[pl.BlockSpec((TM, d), lambda i: (i, 0))] * 2,
        out_shape=[jax.ShapeDtypeStruct((n, d), BF16)] * 2,
        compiler_params=_cparams(("parallel",)),
        name="kv_proj",
    )(mem, w)


def _xattn_kernel(x_ref, k_ref, v_ref, wq_ref, wo_ref, g_ref, b_ref, o_ref, oh_ref):
    x = x_ref[...]
    q = (jnp.dot(x.astype(BF16), wq_ref[...], preferred_element_type=F32)
         * (XA_HEAD ** -0.5)).astype(BF16)
    for h in range(XA_HEADS):
        cols = slice(h * XA_HEAD, (h + 1) * XA_HEAD)
        s = lax.dot_general(q[:, cols], k_ref[:, cols], (((1,), (1,)), ((), ())),
                            preferred_element_type=F32)
        s = s - jnp.max(s, axis=-1, keepdims=True)
        p = jnp.exp(s)
        p = p / jnp.sum(p, axis=-1, keepdims=True)
        oh_ref[:, cols] = jnp.dot(p.astype(BF16), v_ref[:, cols],
                                  preferred_element_type=F32).astype(BF16)
    c = jnp.dot(oh_ref[...], wo_ref[...], preferred_element_type=F32)
    o_ref[...] = _layer_norm(DN_ALPHA * x + c, g_ref[...], b_ref[...])


def xattn_ln(x, k, v, wq, wo, g, b, seq):
    n, d = x.shape
    tiles_per_seq = seq // TM
    kv_spec = pl.BlockSpec((MEM_LEN, d), lambda i: (i // tiles_per_seq, 0))
    return pl.pallas_call(
        _xattn_kernel,
        grid=(n // TM,),
        in_specs=[pl.BlockSpec((TM, d), lambda i: (i, 0)), kv_spec, kv_spec,
                  _const_spec(wq.shape), _const_spec(wo.shape),
                  _const_spec((1, d)), _const_spec((1, d))],
        out_specs=pl.BlockSpec((TM, d), lambda i: (i, 0)),
        out_shape=jax.ShapeDtypeStruct((n, d), F32),
        scratch_shapes=[pltpu.VMEM((TM, d), BF16)],
        compiler_params=_cparams(("parallel",)),
        name="xattn_ln",
    )(x, k, v, wq, wo, g, b)


def _ffn_weights(w_gate, w_up, w_down):
    d, dff = w_gate.shape
    nc = dff // FF_CHUNK
    chunked = lambda w: w.astype(BF16).reshape(d, nc, FF_CHUNK).transpose(1, 0, 2)
    return chunked(w_gate), chunked(w_up), w_down.astype(BF16).reshape(nc, FF_CHUNK, d)


def _lru_gate_matrix(gate_w):
    eye = jnp.eye(LRU_BLOCKS, dtype=F32)
    dense = jnp.einsum('gncd,nm->gncmd', gate_w.astype(F32), eye)
    width = LRU_BLOCKS * LRU_BLOCK
    dense = dense.reshape(2, width, width)
    return jnp.concatenate([dense[0], dense[1]], axis=-1).astype(BF16)


def _diff_lambda_init(layer):
    return 0.8 - 0.6 * math.exp(-0.3 * layer)


EVEN_SECTIONS = (
    (0, HALF_MIX, True, DA_HEAD ** -0.5),
    (HALF_MIX, HALF_MIX, True, 1.0),
    (2 * HALF_MIX, HALF_MIX, False, 1.0),
    (3 * HALF_MIX, HALF_MIX, False, 1.0),
    (4 * HALF_MIX, HALF_MIX, False, 1.0),
)
EVEN_DTYPES = (BF16, BF16, BF16, F32, F32)
RET_QK_W = RET_HEADS * RET_QK
ODD_SECTIONS = (
    (0, HALF_MIX, False, 1.0),
    (HALF_MIX, RET_QK_W, True, 1.0),
    (HALF_MIX + RET_QK_W, RET_QK_W, True, RET_QK ** -0.5),
    (HALF_MIX + 2 * RET_QK_W, HALF_MIX, False, 1.0),
    (2 * HALF_MIX + 2 * RET_QK_W, HALF_MIX, False, 1.0),
)
ODD_DTYPES = (F32, BF16, BF16, BF16, F32)


def kernel(x, mem, ln_g, ln_b, ffn_w_gate, ffn_w_up, ffn_w_down, xa_w_q, xa_w_kv, xa_w_o,
           ev_w_in, ev_w_out, da_lambda, da_norm_g, lru_conv_w, lru_conv_b, lru_gate_w,
           lru_gate_b, lru_lambda, od_w_in, od_w_out, s5_lam_re, s5_lam_im, s5_log_step,
           s5_b_re, s5_b_im, s5_c_re, s5_c_im, s5_d, s5_glu_w, s5_glu_b, ret_norm_g):
    bsz, seq, d = x.shape
    n = bsz * seq
    depth = ln_g.shape[0]
    mem2 = mem.reshape(bsz * mem.shape[1], d)
    da_tables = _rotary_tables(seq, DA_ROPE, ROPE_THETA, DA_HEAD)
    ret_rot_tables = _rotary_tables(seq, RET_QK, RET_THETA, RET_QK)
    ret_tables = _retention_tables()
    row = lambda v: v.reshape(1, -1).astype(F32)

    h = x.reshape(n, d)
    for l in range(depth):
        wg, wu, wd = _ffn_weights(ffn_w_gate[l, 0], ffn_w_up[l, 0], ffn_w_down[l, 0])
        h = ffn_ln(h, wg, wu, wd, row(ln_g[l, 0]), row(ln_b[l, 0]))

        if l % 2 == 0:
            e = l // 2
            q, k, v, gate, xr = in_proj(h, ev_w_in[e].astype(BF16), da_tables, EVEN_SECTIONS,
                                        EVEN_DTYPES, DA_ROPE // 2, seq)
            shp = (bsz, seq, HALF_MIX)
            a_out = diff_attn(q.reshape(shp), k.reshape(shp), v.reshape(shp), da_lambda[e],
                              row(da_norm_g[e]), _diff_lambda_init(l))
            b_out = rglru(xr.reshape(shp), gate.reshape(shp), lru_conv_w[e], row(lru_conv_b[e]),
                          _lru_gate_matrix(lru_gate_w[e]), row(lru_gate_b[e]), row(lru_lambda[e]))
            w_out = ev_w_out[e]
        else:
            o = l // 2
            u, q, k, v, g = in_proj(h, od_w_in[o].astype(BF16), ret_rot_tables, ODD_SECTIONS,
                                    ODD_DTYPES, RET_QK // 2, seq)
            shp = (bsz, seq, HALF_MIX)
            wb, a_re, a_im, wc = _s5_params(s5_lam_re[o], s5_lam_im[o], s5_log_step[o],
                                            s5_b_re[o], s5_b_im[o], s5_c_re[o], s5_c_im[o])
            a_out = s5(u.reshape(shp), wb, a_re, a_im, wc, row(s5_d[o]),
                       s5_glu_w[o].astype(BF16), row(s5_glu_b[o]))
            qk_shp = (bsz, seq, RET_QK_W)
            b_out = retention(q.reshape(qk_shp), k.reshape(qk_shp), v.reshape(shp),
                              g.reshape(shp), row(ret_norm_g[o]), ret_tables)
            w_out = od_w_out[o]
        h = out_proj_ln(h, a_out.reshape(n, HALF_MIX), b_out.reshape(n, HALF_MIX),
                        w_out.astype(BF16), row(ln_g[l, 1]), row(ln_b[l, 1]))

        kk, vv = kv_proj(mem2, xa_w_kv[l].astype(BF16))
        h = xattn_ln(h, kk, vv, xa_w_q[l].astype(BF16), xa_w_o[l].astype(BF16),
                     row(ln_g[l, 2]), row(ln_b[l, 2]), seq)

        wg, wu, wd = _ffn_weights(ffn_w_gate[l, 1], ffn_w_up[l, 1], ffn_w_down[l, 1])
        h = ffn_ln(h, wg, wu, wd, row(ln_g[l, 3]), row(ln_b[l, 3]))
    return h.reshape(bsz, seq, d)
```

```python
import functools
import math

import jax
import jax.numpy as jnp
from jax import lax
from jax.experimental import pallas as pl
from jax.experimental.pallas import tpu as pltpu

F32 = jnp.float32
BF16 = jnp.bfloat16

D_MODEL = 1024
DEPTH = 4
HALF_MIX = 512
DA_HEADS = 4
DA_HEAD = 64
DA_ROPE = 16
ROPE_THETA = 500000.0
LRU_BLOCKS = 8
LRU_BLOCK = 64
CONV_W = 4
LRU_C = 8.0
S5_GROUP = 16
S5_GROUPS = 32
S5_STATE = 64
RET_HEADS = 4
RET_QK = 64
RET_V = 128
RET_THETA = 10000.0
RET_CHUNK = 128
MEM_LEN = 256
XA_HEADS = 4
XA_HEAD = 256
D_FF = 2816
DN_ALPHA = (2 * DEPTH) ** 0.25
LN_EPS = 1e-5

LANES = 128
SUBLANES = 8
VMEM_LIMIT = 56 * 1024 * 1024

TM = 512
FF_CHUNK = 256
TQ = 256
SCAN_BATCH = SUBLANES
LRU_T = 256
S5_T = 128
PITCH_PAD = 4
NEG_BIG = -1e30


def _cparams(sem):
    return pltpu.CompilerParams(dimension_semantics=sem, vmem_limit_bytes=VMEM_LIMIT)


def _const_spec(shape):
    nd = len(shape)
    return pl.BlockSpec(shape, lambda *_: (0,) * nd, pipeline_mode=pl.Buffered(1))


def _layer_norm(y, g, b):
    mu = jnp.mean(y, axis=-1, keepdims=True)
    yc = y - mu
    var = jnp.mean(yc * yc, axis=-1, keepdims=True)
    return yc * lax.rsqrt(var + LN_EPS) * g + b


def _ffn_kernel(x_ref, wg_ref, wu_ref, wd_ref, g_ref, b_ref, o_ref, xb_ref, a_ref):
    xb_ref[...] = x_ref[...].astype(BF16)
    dff = wg_ref.shape[1]
    for c0 in range(0, dff, FF_CHUNK):
        cols = slice(c0, c0 + FF_CHUNK)
        xb = xb_ref[...]
        g = jnp.dot(xb, wg_ref[:, cols], preferred_element_type=F32)
        u = jnp.dot(xb, wu_ref[:, cols], preferred_element_type=F32)
        a_ref[:, cols] = (g * jax.nn.sigmoid(g) * u).astype(BF16)
    half = x_ref.shape[0] // 2
    for r in range(2):
        rows = slice(r * half, (r + 1) * half)
        acc = jnp.dot(a_ref[rows, :], wd_ref[...], preferred_element_type=F32)
        y = DN_ALPHA * x_ref[rows, :] + 0.5 * acc
        o_ref[rows, :] = _layer_norm(y, g_ref[...], b_ref[...])


def ffn_ln(x, wg, wu, wd, g, b):
    n, d = x.shape
    dff = wg.shape[1]
    return pl.pallas_call(
        _ffn_kernel,
        grid=(n // TM,),
        in_specs=[
            pl.BlockSpec((TM, d), lambda i: (i, 0)),
            _const_spec((d, dff)),
            _const_spec((d, dff)),
            _const_spec((dff, d)),
            _const_spec((1, d)),
            _const_spec((1, d)),
        ],
        out_specs=pl.BlockSpec((TM, d), lambda i: (i, 0)),
        out_shape=jax.ShapeDtypeStruct((n, d), F32),
        scratch_shapes=[pltpu.VMEM((TM, d), BF16), pltpu.VMEM((TM, dff), BF16)],
        compiler_params=_cparams(("parallel",)),
        name="ffn_ln",
    )(x, wg, wu, wd, g, b)


def _in_proj_kernel(x_ref, w_ref, cos_ref, sa_ref, sb_ref, *o_refs, sections, rot_shift):
    xb = x_ref[...].astype(BF16)
    for (start, width, rotate, scale), o_ref in zip(sections, o_refs):
        z = jnp.dot(xb, w_ref[:, start:start + width], preferred_element_type=F32)
        if rotate:
            cos, sa, sb = cos_ref[...], sa_ref[...], sb_ref[...]
            for j in range(width // LANES):
                zj = z[:, j * LANES:(j + 1) * LANES]
                zr = (zj * cos + pltpu.roll(zj, LANES - rot_shift, 1) * sa
                      + pltpu.roll(zj, rot_shift, 1) * sb)
                o_ref[:, j * LANES:(j + 1) * LANES] = (zr * scale).astype(o_ref.dtype)
        else:
            o_ref[...] = (z * scale).astype(o_ref.dtype)


def in_proj(x, w, tables, sections, out_dtypes, rot_shift, seq):
    n, d = x.shape
    tiles_per_seq = seq // TM
    tab_spec = pl.BlockSpec((TM, LANES), lambda i: (i % tiles_per_seq, 0))
    return pl.pallas_call(
        functools.partial(_in_proj_kernel, sections=sections, rot_shift=rot_shift),
        grid=(n // TM,),
        in_specs=[pl.BlockSpec((TM, d), lambda i: (i, 0)), _const_spec(w.shape),
                  tab_spec, tab_spec, tab_spec],
        out_specs=[pl.BlockSpec((TM, s[1]), lambda i: (i, 0)) for s in sections],
        out_shape=[jax.ShapeDtypeStruct((n, s[1]), dt) for s, dt in zip(sections, out_dtypes)],
        compiler_params=_cparams(("parallel",)),
        name="in_proj",
    )(x, w, *tables)


def _rotary_tables(seq, rot_dim, theta, group):
    half = rot_dim // 2
    inv = theta ** (-jnp.arange(half, dtype=F32) * 2.0 / rot_dim)
    ang = jnp.arange(seq, dtype=jnp.int32).astype(F32)[:, None] * inv[None, :]
    cos, sin = jnp.cos(ang), jnp.sin(ang)
    zeros = jnp.zeros((seq, group - rot_dim), F32)
    zh = jnp.zeros((seq, half), F32)
    cos_g = jnp.concatenate([cos, cos, jnp.ones((seq, group - rot_dim), F32)], -1)
    sa_g = jnp.concatenate([-sin, zh, zeros], -1)
    sb_g = jnp.concatenate([zh, sin, zeros], -1)
    reps = LANES // group
    return tuple(jnp.tile(t, (1, reps)) for t in (cos_g, sa_g, sb_g))


def _out_proj_kernel(x_ref, a_ref, b_ref, w_ref, g_ref, beta_ref, o_ref):
    half = a_ref.shape[1]
    m = jnp.dot(a_ref[...], w_ref[:half, :], preferred_element_type=F32)
    m = m + jnp.dot(b_ref[...], w_ref[half:, :], preferred_element_type=F32)
    o_ref[...] = _layer_norm(DN_ALPHA * x_ref[...] + m, g_ref[...], beta_ref[...])


def out_proj_ln(x, a, b, w, g, beta):
    n, d = x.shape
    half = a.shape[1]
    return pl.pallas_call(
        _out_proj_kernel,
        grid=(n // TM,),
        in_specs=[pl.BlockSpec((TM, d), lambda i: (i, 0)),
                  pl.BlockSpec((TM, half), lambda i: (i, 0)),
                  pl.BlockSpec((TM, half), lambda i: (i, 0)),
                  _const_spec(w.shape), _const_spec((1, d)), _const_spec((1, d))],
        out_specs=pl.BlockSpec((TM, d), lambda i: (i, 0)),
        out_shape=jax.ShapeDtypeStruct((n, d), F32),
        compiler_params=_cparams(("parallel",)),
        name="out_proj_ln",
    )(x, a, b, w, g, beta)


def _diff_attn_kernel(q_ref, k_ref, v_ref, lam_ref, g_ref, o_ref, vx_ref, s_ref, p_ref,
                      *, lambda_init):
    seq = q_ref.shape[0]
    tq = TQ
    vx_ref[:, :LANES] = v_ref[...]
    vx_ref[:, LANES:] = jnp.ones((seq, LANES), BF16)
    lam = lam_ref[...]
    lmbda = (jnp.exp(jnp.sum(lam[0:1] * lam[1:2], axis=-1, keepdims=True))
             - jnp.exp(jnp.sum(lam[2:3] * lam[3:4], axis=-1, keepdims=True)) + lambda_init)
    lane = lax.broadcasted_iota(jnp.int32, (tq, LANES), 1)
    row = lax.broadcasted_iota(jnp.int32, (2 * tq, tq), 0)
    col = lax.broadcasted_iota(jnp.int32, (2 * tq, tq), 1)
    causal = col <= jnp.where(row >= tq, row - tq, row)

    for qi in range(seq // tq):
        qf = q_ref[qi * tq:(qi + 1) * tq, :].astype(F32)
        qq = jnp.concatenate([jnp.where(lane < DA_HEAD, qf, 0.0),
                              jnp.where(lane >= DA_HEAD, qf, 0.0)], axis=0).astype(BF16)
        m_run = None
        for j in range(qi + 1):
            s = lax.dot_general(qq, k_ref[j * tq:(j + 1) * tq, :], (((1,), (1,)), ((), ())),
                                preferred_element_type=F32)
            if j == qi:
                s = jnp.where(causal, s, NEG_BIG)
            s_ref[:, j * tq:(j + 1) * tq] = s
            for c in range(tq // LANES):
                sc = s[:, c * LANES:(c + 1) * LANES]
                m_run = sc if m_run is None else jnp.maximum(m_run, sc)
        m = jnp.max(m_run, axis=-1, keepdims=True)
        for j in range(qi + 1):
            cols = slice(j * tq, (j + 1) * tq)
            p_ref[:, cols] = jnp.exp2(s_ref[:, cols] - m).astype(BF16)
        kv = (qi + 1) * tq
        pv = jnp.dot(p_ref[:, :kv], vx_ref[:kv, :], preferred_element_type=F32)
        o = pv[:, :LANES] / pv[:, LANES:]
        d = o[:tq] - lmbda * o[tq:]
        rms = lax.rsqrt(jnp.mean(d * d, axis=-1, keepdims=True) + LN_EPS)
        o_ref[qi * tq:(qi + 1) * tq, :] = (
            d * rms * g_ref[...] * (1.0 - lambda_init)).astype(o_ref.dtype)


def diff_attn(q, k, v, lam, norm_g, lambda_init):
    bsz, seq, width = q.shape
    heads = width // LANES
    spec = pl.BlockSpec((None, seq, LANES), lambda b, h: (b, 0, h))
    return pl.pallas_call(
        functools.partial(_diff_attn_kernel, lambda_init=lambda_init),
        grid=(bsz, heads),
        in_specs=[spec, spec, spec, _const_spec(lam.shape), _const_spec(norm_g.shape)],
        out_specs=spec,
        out_shape=jax.ShapeDtypeStruct((bsz, seq, width), BF16),
        scratch_shapes=[pltpu.VMEM((seq, 2 * LANES), BF16),
                        pltpu.VMEM((2 * TQ, seq), F32),
                        pltpu.VMEM((2 * TQ, seq), BF16)],
        compiler_params=_cparams(("parallel", "parallel")),
        name="diff_attn",
    )(q, k, v, lam, norm_g)


def _pitch(t_steps):
    return t_steps + PITCH_PAD


def _rows(t_steps):
    return SCAN_BATCH * _pitch(t_steps)


def _seq_row0(b, t_steps):
    return b * _pitch(t_steps) + PITCH_PAD


def _time_rows(t, t_steps):
    return pl.ds(PITCH_PAD + t, SCAN_BATCH, stride=_pitch(t_steps))


CONV_HEAD = SUBLANES


def _rglru_kernel(xr_ref, gate_ref, cw_ref, cb_ref, wg_ref, gb_ref, lam_ref, o_ref,
                  xpad_ref, a_ref, b_ref, h_ref, *, t_steps, row_chunks):
    ti = pl.program_id(1)
    pitch, rows = _pitch(t_steps), _rows(t_steps)
    width = xr_ref.shape[2]
    nslab = width // LANES
    hist = CONV_W - 1

    @pl.when(ti == 0)
    def _():
        xpad_ref[...] = jnp.zeros_like(xpad_ref)
        h_ref[...] = jnp.zeros_like(h_ref)

    @pl.when(ti > 0)
    def _():
        for b in range(SCAN_BATCH):
            r0 = CONV_HEAD + _seq_row0(b, t_steps)
            xpad_ref[pl.ds(r0 - hist, hist), :] = xpad_ref[pl.ds(r0 + t_steps - hist, hist), :]

    for b in range(SCAN_BATCH):
        xpad_ref[pl.ds(CONV_HEAD + _seq_row0(b, t_steps), t_steps), :] = xr_ref[b]

    softplus = jax.nn.softplus(-lam_ref[...])
    chunk = rows // row_chunks
    for c in range(row_chunks):
        r0 = c * chunk
        xc = cb_ref[...] + cw_ref[0:1, :] * xpad_ref[pl.ds(CONV_HEAD + r0 - hist, chunk), :]
        for j in range(1, CONV_W):
            xc = xc + cw_ref[j:j + 1, :] * xpad_ref[pl.ds(CONV_HEAD + r0 - hist + j, chunk), :]
        gates = jnp.dot(xc.astype(BF16), wg_ref[...], preferred_element_type=F32) + gb_ref[...]
        r = jax.nn.sigmoid(gates[:, :width])
        i = jax.nn.sigmoid(gates[:, width:])
        log_a = -LRU_C * r * softplus
        a = jnp.exp(log_a)
        bb = jnp.sqrt(-jnp.tanh(log_a) * (a * a + 1.0)) * (i * xc)
        for s in range(nslab):
            a_ref[s, pl.ds(r0, chunk), :] = a[:, s * LANES:(s + 1) * LANES]
            b_ref[s, pl.ds(r0, chunk), :] = bb[:, s * LANES:(s + 1) * LANES]

    def step(t, hs):
        new = []
        for s in range(nslab):
            rows_t = _time_rows(t, t_steps)
            h = a_ref[s, rows_t, :] * hs[s] + b_ref[s, rows_t, :]
            b_ref[s, rows_t, :] = h
            new.append(h)
        return tuple(new)

    hs = lax.fori_loop(0, t_steps, step, tuple(h_ref[s] for s in range(nslab)), unroll=8)
    for s in range(nslab):
        h_ref[s] = hs[s]

    for b in range(SCAN_BATCH):
        r0 = _seq_row0(b, t_steps)
        for s in range(nslab):
            h = b_ref[s, pl.ds(r0, t_steps), :]
            gt = gate_ref[b, :, s * LANES:(s + 1) * LANES]
            o_ref[b, :, s * LANES:(s + 1) * LANES] = (jax.nn.gelu(gt) * h).astype(o_ref.dtype)


def rglru(xr, gate, conv_w, conv_b, w_gates, gate_b, lru_lam, t_steps=None):
    bsz, seq, width = xr.shape
    t_steps = t_steps or min(LRU_T, seq)
    rows = _rows(t_steps)
    row_chunks = next(c for c in (5, 4, 3, 2, 1) if (rows // SUBLANES) % c == 0)
    blk = pl.BlockSpec((SCAN_BATCH, t_steps, width), lambda g, t: (g, t, 0))
    return pl.pallas_call(
        functools.partial(_rglru_kernel, t_steps=t_steps, row_chunks=row_chunks),
        grid=(bsz // SCAN_BATCH, seq // t_steps),
        in_specs=[blk, blk, _const_spec(conv_w.shape), _const_spec(conv_b.shape),
                  _const_spec(w_gates.shape), _const_spec(gate_b.shape),
                  _const_spec(lru_lam.shape)],
        out_specs=blk,
        out_shape=jax.ShapeDtypeStruct((bsz, seq, width), BF16),
        scratch_shapes=[pltpu.VMEM((CONV_HEAD + rows, width), F32),
                        pltpu.VMEM((width // LANES, rows, LANES), F32),
                        pltpu.VMEM((width // LANES, rows, LANES), F32),
                        pltpu.VMEM((width // LANES, SCAN_BATCH, LANES), F32)],
        compiler_params=_cparams(("parallel", "arbitrary")),
        name="rglru",
    )(xr, gate, conv_w, conv_b, w_gates, gate_b, lru_lam)


S5_PAIRS = S5_GROUPS // 2
S5_PAIRS_PER_PASS = 4
S5_PAIRS_PER_BLOCK = LANES // (2 * S5_GROUP)


def _s5_kernel(u_ref, wb_ref, are_ref, aim_ref, wc_ref, d_ref, wglu_ref, bglu_ref, o_ref,
               upad_ref, xs_ref, h_ref, y_ref, *, t_steps):
    ti = pl.program_id(1)
    rows = _rows(t_steps)
    width = u_ref.shape[2]

    @pl.when(ti == 0)
    def _():
        upad_ref[...] = jnp.zeros_like(upad_ref)
        h_ref[...] = jnp.zeros_like(h_ref)

    for b in range(SCAN_BATCH):
        upad_ref[pl.ds(_seq_row0(b, t_steps), t_steps), :] = u_ref[b]

    for p in range(S5_PAIRS):
        blk = p // S5_PAIRS_PER_BLOCK
        ub = upad_ref[:, blk * LANES:(blk + 1) * LANES].astype(BF16)
        x = jnp.dot(ub, wb_ref[p], preferred_element_type=F32)
        xs_ref[2 * p] = x[:, :LANES]
        xs_ref[2 * p + 1] = x[:, LANES:]

    for p0 in range(0, S5_PAIRS, S5_PAIRS_PER_PASS):
        pairs = range(p0, p0 + S5_PAIRS_PER_PASS)
        a_re = [jnp.broadcast_to(are_ref[p:p + 1, :], (SCAN_BATCH, LANES)) for p in pairs]
        a_im = [jnp.broadcast_to(aim_ref[p:p + 1, :], (SCAN_BATCH, LANES)) for p in pairs]

        def step(t, hs, pairs=pairs, a_re=a_re, a_im=a_im):
            new = []
            rows_t = _time_rows(t, t_steps)
            for n, p in enumerate(pairs):
                h_re, h_im = hs[2 * n], hs[2 * n + 1]
                n_re = a_re[n] * h_re - a_im[n] * h_im + xs_ref[2 * p, rows_t, :]
                n_im = a_re[n] * h_im + a_im[n] * h_re + xs_ref[2 * p + 1, rows_t, :]
                xs_ref[2 * p, rows_t, :] = n_re
                xs_ref[2 * p + 1, rows_t, :] = n_im
                new += [n_re, n_im]
            return tuple(new)

        init = tuple(h_ref[s] for p in pairs for s in (2 * p, 2 * p + 1))
        hs = lax.fori_loop(0, t_steps, step, init, unroll=4)
        for n, p in enumerate(pairs):
            h_ref[2 * p] = hs[2 * n]
            h_ref[2 * p + 1] = hs[2 * n + 1]

    pairs_per_out = S5_PAIRS // (width // LANES)
    for q in range(width // LANES):
        y = d_ref[:, q * LANES:(q + 1) * LANES] * upad_ref[:, q * LANES:(q + 1) * LANES]
        for p in range(q * pairs_per_out, (q + 1) * pairs_per_out):
            hcat = jnp.concatenate([xs_ref[2 * p], xs_ref[2 * p + 1]], axis=-1).astype(BF16)
            y = y + jnp.dot(hcat, wc_ref[p], preferred_element_type=F32)
        y_ref[:, q * LANES:(q + 1) * LANES] = jax.nn.gelu(y)

    z = y_ref[...]
    gate = jnp.dot(z.astype(BF16), wglu_ref[...], preferred_element_type=F32) + bglu_ref[...]
    y_ref[...] = z * jax.nn.sigmoid(gate)
    for b in range(SCAN_BATCH):
        o_ref[b] = y_ref[pl.ds(_seq_row0(b, t_steps), t_steps), :].astype(o_ref.dtype)


def s5(u, wb, a_re, a_im, wc, d_skip, w_glu, b_glu, t_steps=None):
    bsz, seq, width = u.shape
    t_steps = t_steps or min(S5_T, seq)
    rows = _rows(t_steps)
    blk = pl.BlockSpec((SCAN_BATCH, t_steps, width), lambda g, t: (g, t, 0))
    nslab = 2 * S5_PAIRS
    return pl.pallas_call(
        functools.partial(_s5_kernel, t_steps=t_steps),
        grid=(bsz // SCAN_BATCH, seq // t_steps),
        in_specs=[blk, _const_spec(wb.shape), _const_spec(a_re.shape), _const_spec(a_im.shape),
                  _const_spec(wc.shape), _const_spec(d_skip.shape), _const_spec(w_glu.shape),
                  _const_spec(b_glu.shape)],
        out_specs=blk,
        out_shape=jax.ShapeDtypeStruct((bsz, seq, width), BF16),
        scratch_shapes=[pltpu.VMEM((rows, width), F32),
                        pltpu.VMEM((nslab, rows, LANES), F32),
                        pltpu.VMEM((nslab, SCAN_BATCH, LANES), F32),
                        pltpu.VMEM((rows, width), F32)],
        compiler_params=_cparams(("parallel", "arbitrary")),
        name="s5",
    )(u, wb, a_re, a_im, wc, d_skip, w_glu, b_glu)


def _s5_params(lam_re, lam_im, log_step, b_re, b_im, c_re, c_im):
    step = jnp.exp(log_step.astype(F32))[:, None]
    lr = jnp.minimum(lam_re.astype(F32), -1e-4)
    li = lam_im.astype(F32)
    mag = jnp.exp(lr * step)
    ang = li * step
    ab_re, ab_im = mag * jnp.cos(ang), mag * jnp.sin(ang)
    den = lr * lr + li * li
    nr, ni = ab_re - 1.0, ab_im
    f_re = (nr * lr + ni * li) / den
    f_im = (ni * lr - nr * li) / den
    bb_re = f_re[:, :, None] * b_re - f_im[:, :, None] * b_im
    bb_im = f_re[:, :, None] * b_im + f_im[:, :, None] * b_re
    g, n, c = bb_re.shape
    eye2 = jnp.eye(2, dtype=F32)
    eye_b = jnp.eye(S5_PAIRS_PER_BLOCK, dtype=F32)

    def in_mat(bb):
        bt = bb.transpose(0, 2, 1).reshape(S5_PAIRS, 2, c, n)
        m = jnp.einsum('pgcn,gh->pgchn', bt, eye2).reshape(S5_PAIRS, 2 * c, 2 * n)
        slot = jnp.arange(S5_PAIRS) % S5_PAIRS_PER_BLOCK
        sel = eye_b[slot]
        return jnp.einsum('pkn,ps->pskn', m, sel).reshape(S5_PAIRS, LANES, 2 * n)

    wb = jnp.concatenate([in_mat(bb_re), in_mat(bb_im)], axis=-1).astype(BF16)

    pairs_per_out = S5_PAIRS // (HALF_MIX // LANES)
    eye_o = jnp.eye(pairs_per_out, dtype=F32)

    def out_mat(cc):
        ct = cc.transpose(0, 2, 1).reshape(S5_PAIRS, 2, n, c)
        m = jnp.einsum('pgnc,gh->pgnhc', ct, eye2).reshape(S5_PAIRS, 2 * n, 2 * c)
        slot = jnp.arange(S5_PAIRS) % pairs_per_out
        sel = eye_o[slot]
        return jnp.einsum('pnk,ps->pnsk', m, sel).reshape(S5_PAIRS, 2 * n, LANES)

    wc = jnp.concatenate([out_mat(c_re.astype(F32)), -out_mat(c_im.astype(F32))],
                         axis=1).astype(BF16)
    a_re = ab_re.reshape(S5_PAIRS, 2 * n)
    a_im = ab_im.reshape(S5_PAIRS, 2 * n)
    return wb, a_re, a_im, wc


def _retention_kernel(q_ref, k_ref, v_ref, g_ref, ng_ref, din_ref, xi_ref, zeta_ref, cd_ref,
                      o_ref, r_ref):
    h = pl.program_id(1)
    seq = q_ref.shape[0]
    c = RET_CHUNK
    lane = lax.broadcasted_iota(jnp.int32, (c, LANES), 1)
    mine = (lane // RET_QK) == (h % (LANES // RET_QK))
    r_ref[...] = jnp.zeros_like(r_ref)

    def body(ci, carry):
        rows = pl.ds(pl.multiple_of(ci * c, c), c)
        qm = jnp.where(mine, q_ref[rows, :].astype(F32), 0.0)
        km = jnp.where(mine, k_ref[rows, :].astype(F32), 0.0)
        qb = qm.astype(BF16)
        vb = v_ref[rows, :]
        inner = lax.dot_general(qb, km.astype(BF16), (((1,), (1,)), ((), ())),
                                preferred_element_type=F32) * din_ref[...]
        o = jnp.dot(inner.astype(BF16), vb, preferred_element_type=F32)
        o = o + jnp.dot(qb, r_ref[...].astype(BF16), preferred_element_type=F32) * xi_ref[...]
        kz = (km * zeta_ref[...]).astype(BF16)
        r_ref[...] = cd_ref[...] * r_ref[...] + lax.dot_general(
            kz, vb, (((0,), (0,)), ((), ())), preferred_element_type=F32)
        mu = jnp.mean(o, axis=-1, keepdims=True)
        oc = o - mu
        var = jnp.mean(oc * oc, axis=-1, keepdims=True)
        on = oc * lax.rsqrt(var + LN_EPS) * ng_ref[...]
        gt = g_ref[rows, :]
        o_ref[rows, :] = (gt * jax.nn.sigmoid(gt) * on).astype(o_ref.dtype)
        return carry

    lax.fori_loop(0, seq // c, body, 0)


def retention(q, k, v, g, norm_g, tables):
    bsz, seq, _ = q.shape
    width = v.shape[2]
    heads = width // LANES
    heads_per_blk = LANES // RET_QK
    qk_spec = pl.BlockSpec((None, seq, LANES), lambda b, h: (b, 0, h // heads_per_blk))
    v_spec = pl.BlockSpec((None, seq, LANES), lambda b, h: (b, 0, h))
    tab_spec = pl.BlockSpec((None, RET_CHUNK, LANES), lambda b, h: (h, 0, 0))
    cd_spec = pl.BlockSpec((None, 1, LANES), lambda b, h: (h, 0, 0))
    return pl.pallas_call(
        _retention_kernel,
        grid=(bsz, heads),
        in_specs=[qk_spec, qk_spec, v_spec, v_spec, _const_spec(norm_g.shape),
                  tab_spec, tab_spec, tab_spec, cd_spec],
        out_specs=v_spec,
        out_shape=jax.ShapeDtypeStruct((bsz, seq, width), BF16),
        scratch_shapes=[pltpu.VMEM((LANES, LANES), F32)],
        compiler_params=_cparams(("parallel", "arbitrary")),
        name="retention",
    )(q, k, v, g, norm_g, *tables)


def _retention_tables():
    c = RET_CHUNK
    log_g = jnp.log(1.0 - jnp.exp2(-5.0 - jnp.arange(RET_HEADS, dtype=F32)))
    idx = jnp.arange(c, dtype=F32)
    diff = idx[:, None] - idx[None, :]
    causal = diff >= 0
    decay_in = jnp.where(causal[None], jnp.exp(log_g[:, None, None] * jnp.where(causal, diff, 0.0)[None]), 0.0)
    xi = jnp.exp(log_g[:, None] * (idx + 1.0))
    zeta = jnp.exp(log_g[:, None] * (c - 1.0 - idx))
    chunk_decay = jnp.exp(log_g * c)
    bc = lambda t: jnp.broadcast_to(t[:, :, None], (RET_HEADS, c, LANES))
    cd = jnp.broadcast_to(chunk_decay[:, None, None], (RET_HEADS, 1, LANES))
    return decay_in, bc(xi), bc(zeta), cd


def _kv_proj_kernel(m_ref, w_ref, k_ref, v_ref):
    mb = m_ref[...].astype(BF16)
    d = k_ref.shape[1]
    k_ref[...] = jnp.dot(mb, w_ref[:, :d], preferred_element_type=F32).astype(k_ref.dtype)
    v_ref[...] = jnp.dot(mb, w_ref[:, d:], preferred_element_type=F32).astype(v_ref.dtype)


def kv_proj(mem, w):
    n, d = mem.shape
    return pl.pallas_call(
        _kv_proj_kernel,
        grid=(n // TM,),
        in_specs=[pl.BlockSpec((TM, d), lambda i: (i, 0)), _const_spec(w.shape)],
        out_specs=[pl.BlockSpec((TM, d), lambda i: (i, 0))] * 2,
        out_shape=[jax.ShapeDtypeStruct((n, d), BF16)] * 2,
        compiler_params=_cparams(("parallel",)),
        name="kv_proj",
    )(mem, w)


def _xattn_kernel(x_ref, k_ref, v_ref, wq_ref, wo_ref, g_ref, b_ref, o_ref, oh_ref):
    x = x_ref[...]
    q = (jnp.dot(x.astype(BF16), wq_ref[...], preferred_element_type=F32)
         * (XA_HEAD ** -0.5)).astype(BF16)
    for h in range(XA_HEADS):
        cols = slice(h * XA_HEAD, (h + 1) * XA_HEAD)
        s = lax.dot_general(q[:, cols], k_ref[:, cols], (((1,), (1,)), ((), ())),
                            preferred_element_type=F32)
        s = s - jnp.max(s, axis=-1, keepdims=True)
        p = jnp.exp(s)
        p = p / jnp.sum(p, axis=-1, keepdims=True)
        oh_ref[:, cols] = jnp.dot(p.astype(BF16), v_ref[:, cols],
                                  preferred_element_type=F32).astype(BF16)
    c = jnp.dot(oh_ref[...], wo_ref[...], preferred_element_type=F32)
    o_ref[...] = _layer_norm(DN_ALPHA * x + c, g_ref[...], b_ref[...])


def xattn_ln(x, k, v, wq, wo, g, b, seq):
    n, d = x.shape
    tiles_per_seq = seq // TM
    kv_spec = pl.BlockSpec((MEM_LEN, d), lambda i: (i // tiles_per_seq, 0))
    return pl.pallas_call(
        _xattn_kernel,
        grid=(n // TM,),
        in_specs=[pl.BlockSpec((TM, d), lambda i: (i, 0)), kv_spec, kv_spec,
                  _const_spec(wq.shape), _const_spec(wo.shape),
                  _const_spec((1, d)), _const_spec((1, d))],
        out_specs=pl.BlockSpec((TM, d), lambda i: (i, 0)),
        out_shape=jax.ShapeDtypeStruct((n, d), F32),
        scratch_shapes=[pltpu.VMEM((TM, d), BF16)],
        compiler_params=_cparams(("parallel",)),
        name="xattn_ln",
    )(x, k, v, wq, wo, g, b)


def _ffn_weights(w_gate, w_up, w_down):
    return w_gate.astype(BF16), w_up.astype(BF16), w_down.astype(BF16)


def _lru_gate_matrix(gate_w):
    eye = jnp.eye(LRU_BLOCKS, dtype=F32)
    dense = jnp.einsum('gncd,nm->gncmd', gate_w.astype(F32), eye)
    width = LRU_BLOCKS * LRU_BLOCK
    dense = dense.reshape(2, width, width)
    return jnp.concatenate([dense[0], dense[1]], axis=-1).astype(BF16)


def _diff_lambda_init(layer):
    return 0.8 - 0.6 * math.exp(-0.3 * layer)


EVEN_SECTIONS = (
    (0, HALF_MIX, True, DA_HEAD ** -0.5 * math.log2(math.e)),
    (HALF_MIX, HALF_MIX, True, 1.0),
    (2 * HALF_MIX, HALF_MIX, False, 1.0),
    (3 * HALF_MIX, HALF_MIX, False, 1.0),
    (4 * HALF_MIX, HALF_MIX, False, 1.0),
)
EVEN_DTYPES = (BF16, BF16, BF16, F32, F32)
RET_QK_W = RET_HEADS * RET_QK
ODD_SECTIONS = (
    (0, HALF_MIX, False, 1.0),
    (HALF_MIX, RET_QK_W, True, 1.0),
    (HALF_MIX + RET_QK_W, RET_QK_W, True, RET_QK ** -0.5),
    (HALF_MIX + 2 * RET_QK_W, HALF_MIX, False, 1.0),
    (2 * HALF_MIX + 2 * RET_QK_W, HALF_MIX, False, 1.0),
)
ODD_DTYPES = (F32, BF16, BF16, BF16, F32)


def kernel(x, mem, ln_g, ln_b, ffn_w_gate, ffn_w_up, ffn_w_down, xa_w_q, xa_w_kv, xa_w_o,
           ev_w_in, ev_w_out, da_lambda, da_norm_g, lru_conv_w, lru_conv_b, lru_gate_w,
           lru_gate_b, lru_lambda, od_w_in, od_w_out, s5_lam_re, s5_lam_im, s5_log_step,
           s5_b_re, s5_b_im, s5_c_re, s5_c_im, s5_d, s5_glu_w, s5_glu_b, ret_norm_g):
    bsz, seq, d = x.shape
    n = bsz * seq
    depth = ln_g.shape[0]
    mem2 = mem.reshape(bsz * mem.shape[1], d)
    da_tables = _rotary_tables(seq, DA_ROPE, ROPE_THETA, DA_HEAD)
    ret_rot_tables = _rotary_tables(seq, RET_QK, RET_THETA, RET_QK)
    ret_tables = _retention_tables()
    row = lambda v: v.reshape(1, -1).astype(F32)

    h = x.reshape(n, d)
    for l in range(depth):
        wg, wu, wd = _ffn_weights(ffn_w_gate[l, 0], ffn_w_up[l, 0], ffn_w_down[l, 0])
        h = ffn_ln(h, wg, wu, wd, row(ln_g[l, 0]), row(ln_b[l, 0]))

        if l % 2 == 0:
            e = l // 2
            q, k, v, gate, xr = in_proj(h, ev_w_in[e].astype(BF16), da_tables, EVEN_SECTIONS,
                                        EVEN_DTYPES, DA_ROPE // 2, seq)
            shp = (bsz, seq, HALF_MIX)
            a_out = diff_attn(q.reshape(shp), k.reshape(shp), v.reshape(shp), da_lambda[e],
                              row(da_norm_g[e]), _diff_lambda_init(l))
            b_out = rglru(xr.reshape(shp), gate.reshape(shp), lru_conv_w[e], row(lru_conv_b[e]),
                          _lru_gate_matrix(lru_gate_w[e]), row(lru_gate_b[e]), row(lru_lambda[e]))
            w_out = ev_w_out[e]
        else:
            o = l // 2
            u, q, k, v, g = in_proj(h, od_w_in[o].astype(BF16), ret_rot_tables, ODD_SECTIONS,
                                    ODD_DTYPES, RET_QK // 2, seq)
            shp = (bsz, seq, HALF_MIX)
            wb, a_re, a_im, wc = _s5_params(s5_lam_re[o], s5_lam_im[o], s5_log_step[o],
                                            s5_b_re[o], s5_b_im[o], s5_c_re[o], s5_c_im[o])
            a_out = s5(u.reshape(shp), wb, a_re, a_im, wc, row(s5_d[o]),
                       s5_glu_w[o].astype(BF16), row(s5_glu_b[o]))
            qk_shp = (bsz, seq, RET_QK_W)
            b_out = retention(q.reshape(qk_shp), k.reshape(qk_shp), v.reshape(shp),
                              g.reshape(shp), row(ret_norm_g[o]), ret_tables)
            w_out = od_w_out[o]
        h = out_proj_ln(h, a_out.reshape(n, HALF_MIX), b_out.reshape(n, HALF_MIX),
                        w_out.astype(BF16), row(ln_g[l, 1]), row(ln_b[l, 1]))

        kk, vv = kv_proj(mem2, xa_w_kv[l].astype(BF16))
        h = xattn_ln(h, kk, vv, xa_w_q[l].astype(BF16), xa_w_o[l].astype(BF16),
                     row(ln_g[l, 2]), row(ln_b[l, 2]), seq)

        wg, wu, wd = _ffn_weights(ffn_w_gate[l, 1], ffn_w_up[l, 1], ffn_w_down[l, 1])
        h = ffn_ln(h, wg, wu, wd, row(ln_g[l, 3]), row(ln_b[l, 3]))
    return h.reshape(bsz, seq, d)
```

```python
import functools
import math

import jax
import jax.numpy as jnp
from jax import lax
from jax.experimental import pallas as pl
from jax.experimental.pallas import tpu as pltpu

F32 = jnp.float32
BF16 = jnp.bfloat16

D_MODEL = 1024
DEPTH = 4
HALF_MIX = 512
DA_HEADS = 4
DA_HEAD = 64
DA_ROPE = 16
ROPE_THETA = 500000.0
LRU_BLOCKS = 8
LRU_BLOCK = 64
CONV_W = 4
LRU_C = 8.0
S5_GROUP = 16
S5_GROUPS = 32
S5_STATE = 64
RET_HEADS = 4
RET_QK = 64
RET_V = 128
RET_THETA = 10000.0
RET_CHUNK = 128
MEM_LEN = 256
XA_HEADS = 4
XA_HEAD = 256
D_FF = 2816
DN_ALPHA = (2 * DEPTH) ** 0.25
LN_EPS = 1e-5

LANES = 128
SUBLANES = 8
VMEM_LIMIT = 56 * 1024 * 1024

TM = 512
TM_LN = 1024
FF_CHUNK = 256
TQ = 256
SCAN_BATCH = SUBLANES
LRU_T = 256
S5_T = 128
PITCH_PAD = 4
NEG_BIG = -1e30


def _cparams(sem):
    return pltpu.CompilerParams(dimension_semantics=sem, vmem_limit_bytes=VMEM_LIMIT)


def _const_spec(shape):
    nd = len(shape)
    return pl.BlockSpec(shape, lambda *_: (0,) * nd, pipeline_mode=pl.Buffered(1))


def _layer_norm(y, g, b):
    mu = jnp.mean(y, axis=-1, keepdims=True)
    yc = y - mu
    var = jnp.mean(yc * yc, axis=-1, keepdims=True)
    return yc * lax.rsqrt(var + LN_EPS) * g + b


def _ffn_kernel(x_ref, wg_ref, wu_ref, wd_ref, g_ref, b_ref, o_ref, xb_ref, a_ref):
    xb_ref[...] = x_ref[...].astype(BF16)
    dff = wg_ref.shape[1]
    for c0 in range(0, dff, FF_CHUNK):
        cols = slice(c0, c0 + FF_CHUNK)
        xb = xb_ref[...]
        g = jnp.dot(xb, wg_ref[:, cols], preferred_element_type=F32)
        u = jnp.dot(xb, wu_ref[:, cols], preferred_element_type=F32)
        a_ref[:, cols] = (g * jax.nn.sigmoid(g) * u).astype(BF16)
    half = x_ref.shape[0] // 2
    for r in range(2):
        rows = slice(r * half, (r + 1) * half)
        acc = jnp.dot(a_ref[rows, :], wd_ref[...], preferred_element_type=F32)
        y = DN_ALPHA * x_ref[rows, :] + 0.5 * acc
        o_ref[rows, :] = _layer_norm(y, g_ref[...], b_ref[...])


def ffn_ln(x, wg, wu, wd, g, b):
    n, d = x.shape
    dff = wg.shape[1]
    tm = min(TM_LN, n)
    return pl.pallas_call(
        _ffn_kernel,
        grid=(n // tm,),
        in_specs=[
            pl.BlockSpec((tm, d), lambda i: (i, 0)),
            _const_spec((d, dff)),
            _const_spec((d, dff)),
            _const_spec((dff, d)),
            _const_spec((1, d)),
            _const_spec((1, d)),
        ],
        out_specs=pl.BlockSpec((tm, d), lambda i: (i, 0)),
        out_shape=jax.ShapeDtypeStruct((n, d), F32),
        scratch_shapes=[pltpu.VMEM((tm, d), BF16), pltpu.VMEM((tm, dff), BF16)],
        compiler_params=_cparams(("parallel",)),
        name="ffn_ln",
    )(x, wg, wu, wd, g, b)


def _in_proj_kernel(x_ref, w_ref, cos_ref, sa_ref, sb_ref, *o_refs, sections, rot_shift):
    xb = x_ref[...].astype(BF16)
    for (start, width, rotate, scale), o_ref in zip(sections, o_refs):
        z = jnp.dot(xb, w_ref[:, start:start + width], preferred_element_type=F32)
        if rotate:
            cos, sa, sb = cos_ref[...], sa_ref[...], sb_ref[...]
            for j in range(width // LANES):
                zj = z[:, j * LANES:(j + 1) * LANES]
                zr = (zj * cos + pltpu.roll(zj, LANES - rot_shift, 1) * sa
                      + pltpu.roll(zj, rot_shift, 1) * sb)
                o_ref[:, j * LANES:(j + 1) * LANES] = (zr * scale).astype(o_ref.dtype)
        else:
            o_ref[...] = (z * scale).astype(o_ref.dtype)


def in_proj(x, w, tables, sections, out_dtypes, rot_shift, seq):
    n, d = x.shape
    tiles_per_seq = seq // TM
    tab_spec = pl.BlockSpec((TM, LANES), lambda i: (i % tiles_per_seq, 0))
    return pl.pallas_call(
        functools.partial(_in_proj_kernel, sections=sections, rot_shift=rot_shift),
        grid=(n // TM,),
        in_specs=[pl.BlockSpec((TM, d), lambda i: (i, 0)), _const_spec(w.shape),
                  tab_spec, tab_spec, tab_spec],
        out_specs=[pl.BlockSpec((TM, s[1]), lambda i: (i, 0)) for s in sections],
        out_shape=[jax.ShapeDtypeStruct((n, s[1]), dt) for s, dt in zip(sections, out_dtypes)],
        compiler_params=_cparams(("parallel",)),
        name="in_proj",
    )(x, w, *tables)


def _rotary_tables(seq, rot_dim, theta, group):
    half = rot_dim // 2
    inv = theta ** (-jnp.arange(half, dtype=F32) * 2.0 / rot_dim)
    ang = jnp.arange(seq, dtype=jnp.int32).astype(F32)[:, None] * inv[None, :]
    cos, sin = jnp.cos(ang), jnp.sin(ang)
    zeros = jnp.zeros((seq, group - rot_dim), F32)
    zh = jnp.zeros((seq, half), F32)
    cos_g = jnp.concatenate([cos, cos, jnp.ones((seq, group - rot_dim), F32)], -1)
    sa_g = jnp.concatenate([-sin, zh, zeros], -1)
    sb_g = jnp.concatenate([zh, sin, zeros], -1)
    reps = LANES // group
    return tuple(jnp.tile(t, (1, reps)) for t in (cos_g, sa_g, sb_g))


def _out_proj_kernel(x_ref, a_ref, b_ref, w_ref, g_ref, beta_ref, o_ref):
    half = a_ref.shape[1]
    hrows = x_ref.shape[0] // 2
    for r in range(2):
        rows = slice(r * hrows, (r + 1) * hrows)
        m = jnp.dot(a_ref[rows, :], w_ref[:half, :], preferred_element_type=F32)
        m = m + jnp.dot(b_ref[rows, :], w_ref[half:, :], preferred_element_type=F32)
        o_ref[rows, :] = _layer_norm(DN_ALPHA * x_ref[rows, :] + m, g_ref[...], beta_ref[...])


def out_proj_ln(x, a, b, w, g, beta):
    n, d = x.shape
    half = a.shape[1]
    tm = min(TM_LN, n)
    return pl.pallas_call(
        _out_proj_kernel,
        grid=(n // tm,),
        in_specs=[pl.BlockSpec((tm, d), lambda i: (i, 0)),
                  pl.BlockSpec((tm, half), lambda i: (i, 0)),
                  pl.BlockSpec((tm, half), lambda i: (i, 0)),
                  _const_spec(w.shape), _const_spec((1, d)), _const_spec((1, d))],
        out_specs=pl.BlockSpec((tm, d), lambda i: (i, 0)),
        out_shape=jax.ShapeDtypeStruct((n, d), F32),
        compiler_params=_cparams(("parallel",)),
        name="out_proj_ln",
    )(x, a, b, w, g, beta)


def _diff_attn_kernel(q_ref, k_ref, v_ref, lam_ref, g_ref, o_ref, vx_ref, s_ref, p_ref,
                      *, lambda_init):
    seq = q_ref.shape[0]
    tq = TQ
    vx_ref[:, :LANES] = v_ref[...]
    vx_ref[:, LANES:] = jnp.ones((seq, LANES), BF16)
    lam = lam_ref[...]
    lmbda = (jnp.exp(jnp.sum(lam[0:1] * lam[1:2], axis=-1, keepdims=True))
             - jnp.exp(jnp.sum(lam[2:3] * lam[3:4], axis=-1, keepdims=True)) + lambda_init)
    lane = lax.broadcasted_iota(jnp.int32, (tq, LANES), 1)
    row = lax.broadcasted_iota(jnp.int32, (2 * tq, tq), 0)
    col = lax.broadcasted_iota(jnp.int32, (2 * tq, tq), 1)
    causal = col <= jnp.where(row >= tq, row - tq, row)

    for n, qi in enumerate(reversed(range(seq // tq))):
        s_buf, p_buf = s_ref.at[n % 2], p_ref.at[n % 2]
        qf = q_ref[qi * tq:(qi + 1) * tq, :].astype(F32)
        qq = jnp.concatenate([jnp.where(lane < DA_HEAD, qf, 0.0),
                              jnp.where(lane >= DA_HEAD, qf, 0.0)], axis=0).astype(BF16)
        m_run = None
        for j in range(qi + 1):
            s = lax.dot_general(qq, k_ref[j * tq:(j + 1) * tq, :], (((1,), (1,)), ((), ())),
                                preferred_element_type=F32)
            if j == qi:
                s = jnp.where(causal, s, NEG_BIG)
            s_buf[:, j * tq:(j + 1) * tq] = s
            for c in range(tq // LANES):
                sc = s[:, c * LANES:(c + 1) * LANES]
                m_run = sc if m_run is None else jnp.maximum(m_run, sc)
        m = jnp.max(m_run, axis=-1, keepdims=True)
        for j in range(qi + 1):
            cols = slice(j * tq, (j + 1) * tq)
            p_buf[:, cols] = jnp.exp2(s_buf[:, cols] - m).astype(BF16)
        kv = (qi + 1) * tq
        pv = jnp.dot(p_buf[:, :kv], vx_ref[:kv, :], preferred_element_type=F32)
        o = pv[:, :LANES] / pv[:, LANES:]
        d = o[:tq] - lmbda * o[tq:]
        rms = lax.rsqrt(jnp.mean(d * d, axis=-1, keepdims=True) + LN_EPS)
        o_ref[qi * tq:(qi + 1) * tq, :] = (
            d * rms * g_ref[...] * (1.0 - lambda_init)).astype(o_ref.dtype)


def diff_attn(q, k, v, lam, norm_g, lambda_init):
    bsz, seq, width = q.shape
    heads = width // LANES
    spec = pl.BlockSpec((None, seq, LANES), lambda b, h: (b, 0, h))
    return pl.pallas_call(
        functools.partial(_diff_attn_kernel, lambda_init=lambda_init),
        grid=(bsz, heads),
        in_specs=[spec, spec, spec, _const_spec(lam.shape), _const_spec(norm_g.shape)],
        out_specs=spec,
        out_shape=jax.ShapeDtypeStruct((bsz, seq, width), BF16),
        scratch_shapes=[pltpu.VMEM((seq, 2 * LANES), BF16),
                        pltpu.VMEM((2, 2 * TQ, seq), F32),
                        pltpu.VMEM((2, 2 * TQ, seq), BF16)],
        compiler_params=_cparams(("parallel", "parallel")),
        name="diff_attn",
    )(q, k, v, lam, norm_g)


def _pitch(t_steps):
    return t_steps + PITCH_PAD


def _rows(t_steps):
    return SCAN_BATCH * _pitch(t_steps)


def _seq_row0(b, t_steps):
    return b * _pitch(t_steps) + PITCH_PAD


def _time_rows(t, t_steps):
    return pl.ds(PITCH_PAD + t, SCAN_BATCH, stride=_pitch(t_steps))


CONV_HEAD = SUBLANES


def _rglru_kernel(xr_ref, gate_ref, cw_ref, cb_ref, wg_ref, gb_ref, lam_ref, o_ref,
                  xpad_ref, a_ref, b_ref, h_ref, *, t_steps, row_chunks):
    ti = pl.program_id(1)
    pitch, rows = _pitch(t_steps), _rows(t_steps)
    width = xr_ref.shape[2]
    nslab = width // LANES
    hist = CONV_W - 1

    @pl.when(ti == 0)
    def _():
        xpad_ref[...] = jnp.zeros_like(xpad_ref)
        h_ref[...] = jnp.zeros_like(h_ref)

    @pl.when(ti > 0)
    def _():
        for b in range(SCAN_BATCH):
            r0 = CONV_HEAD + _seq_row0(b, t_steps)
            xpad_ref[pl.ds(r0 - hist, hist), :] = xpad_ref[pl.ds(r0 + t_steps - hist, hist), :]

    for b in range(SCAN_BATCH):
        xpad_ref[pl.ds(CONV_HEAD + _seq_row0(b, t_steps), t_steps), :] = xr_ref[b]

    softplus = jax.nn.softplus(-lam_ref[...])
    chunk = rows // row_chunks
    for c in range(row_chunks):
        r0 = c * chunk
        xc = cb_ref[...] + cw_ref[0:1, :] * xpad_ref[pl.ds(CONV_HEAD + r0 - hist, chunk), :]
        for j in range(1, CONV_W):
            xc = xc + cw_ref[j:j + 1, :] * xpad_ref[pl.ds(CONV_HEAD + r0 - hist + j, chunk), :]
        gates = jnp.dot(xc.astype(BF16), wg_ref[...], preferred_element_type=F32) + gb_ref[...]
        r = jax.nn.sigmoid(gates[:, :width])
        i = jax.nn.sigmoid(gates[:, width:])
        log_a = -LRU_C * r * softplus
        a = jnp.exp(log_a)
        bb = jnp.sqrt(-jnp.tanh(log_a) * (a * a + 1.0)) * (i * xc)
        for s in range(nslab):
            a_ref[s, pl.ds(r0, chunk), :] = a[:, s * LANES:(s + 1) * LANES]
            b_ref[s, pl.ds(r0, chunk), :] = bb[:, s * LANES:(s + 1) * LANES]

    def step(t, hs):
        new = []
        for s in range(nslab):
            rows_t = _time_rows(t, t_steps)
            h = a_ref[s, rows_t, :] * hs[s] + b_ref[s, rows_t, :]
            b_ref[s, rows_t, :] = h
            new.append(h)
        return tuple(new)

    hs = lax.fori_loop(0, t_steps, step, tuple(h_ref[s] for s in range(nslab)), unroll=8)
    for s in range(nslab):
        h_ref[s] = hs[s]

    for b in range(SCAN_BATCH):
        r0 = _seq_row0(b, t_steps)
        for s in range(nslab):
            h = b_ref[s, pl.ds(r0, t_steps), :]
            gt = gate_ref[b, :, s * LANES:(s + 1) * LANES]
            o_ref[b, :, s * LANES:(s + 1) * LANES] = (jax.nn.gelu(gt) * h).astype(o_ref.dtype)


def rglru(xr, gate, conv_w, conv_b, w_gates, gate_b, lru_lam, t_steps=None):
    bsz, seq, width = xr.shape
    t_steps = t_steps or min(LRU_T, seq)
    rows = _rows(t_steps)
    row_chunks = next(c for c in (5, 4, 3, 2, 1) if (rows // SUBLANES) % c == 0)
    blk = pl.BlockSpec((SCAN_BATCH, t_steps, width), lambda g, t: (g, t, 0))
    return pl.pallas_call(
        functools.partial(_rglru_kernel, t_steps=t_steps, row_chunks=row_chunks),
        grid=(bsz // SCAN_BATCH, seq // t_steps),
        in_specs=[blk, blk, _const_spec(conv_w.shape), _const_spec(conv_b.shape),
                  _const_spec(w_gates.shape), _const_spec(gate_b.shape),
                  _const_spec(lru_lam.shape)],
        out_specs=blk,
        out_shape=jax.ShapeDtypeStruct((bsz, seq, width), BF16),
        scratch_shapes=[pltpu.VMEM((CONV_HEAD + rows, width), F32),
                        pltpu.VMEM((width // LANES, rows, LANES), F32),
                        pltpu.VMEM((width // LANES, rows, LANES), F32),
                        pltpu.VMEM((width // LANES, SCAN_BATCH, LANES), F32)],
        compiler_params=_cparams(("parallel", "arbitrary")),
        name="rglru",
    )(xr, gate, conv_w, conv_b, w_gates, gate_b, lru_lam)


S5_PAIRS = S5_GROUPS // 2
S5_PAIRS_PER_PASS = 4
S5_PAIRS_PER_BLOCK = LANES // (2 * S5_GROUP)


def _s5_kernel(u_ref, wb_ref, are_ref, aim_ref, wc_ref, d_ref, wglu_ref, bglu_ref, o_ref,
               upad_ref, xs_ref, h_ref, y_ref, *, t_steps):
    ti = pl.program_id(1)
    rows = _rows(t_steps)
    width = u_ref.shape[2]

    @pl.when(ti == 0)
    def _():
        upad_ref[...] = jnp.zeros_like(upad_ref)
        h_ref[...] = jnp.zeros_like(h_ref)

    for b in range(SCAN_BATCH):
        upad_ref[pl.ds(_seq_row0(b, t_steps), t_steps), :] = u_ref[b]

    for p in range(S5_PAIRS):
        blk = p // S5_PAIRS_PER_BLOCK
        ub = upad_ref[:, blk * LANES:(blk + 1) * LANES].astype(BF16)
        x = jnp.dot(ub, wb_ref[p], preferred_element_type=F32)
        xs_ref[2 * p] = x[:, :LANES]
        xs_ref[2 * p + 1] = x[:, LANES:]

    for p0 in range(0, S5_PAIRS, S5_PAIRS_PER_PASS):
        pairs = range(p0, p0 + S5_PAIRS_PER_PASS)
        a_re = [jnp.broadcast_to(are_ref[p:p + 1, :], (SCAN_BATCH, LANES)) for p in pairs]
        a_im = [jnp.broadcast_to(aim_ref[p:p + 1, :], (SCAN_BATCH, LANES)) for p in pairs]

        def step(t, hs, pairs=pairs, a_re=a_re, a_im=a_im):
            new = []
            rows_t = _time_rows(t, t_steps)
            for n, p in enumerate(pairs):
                h_re, h_im = hs[2 * n], hs[2 * n + 1]
                n_re = a_re[n] * h_re - a_im[n] * h_im + xs_ref[2 * p, rows_t, :]
                n_im = a_re[n] * h_im + a_im[n] * h_re + xs_ref[2 * p + 1, rows_t, :]
                xs_ref[2 * p, rows_t, :] = n_re
                xs_ref[2 * p + 1, rows_t, :] = n_im
                new += [n_re, n_im]
            return tuple(new)

        init = tuple(h_ref[s] for p in pairs for s in (2 * p, 2 * p + 1))
        hs = lax.fori_loop(0, t_steps, step, init, unroll=4)
        for n, p in enumerate(pairs):
            h_ref[2 * p] = hs[2 * n]
            h_ref[2 * p + 1] = hs[2 * n + 1]

    pairs_per_out = S5_PAIRS // (width // LANES)
    for q in range(width // LANES):
        y = d_ref[:, q * LANES:(q + 1) * LANES] * upad_ref[:, q * LANES:(q + 1) * LANES]
        for p in range(q * pairs_per_out, (q + 1) * pairs_per_out):
            hcat = jnp.concatenate([xs_ref[2 * p], xs_ref[2 * p + 1]], axis=-1).astype(BF16)
            y = y + jnp.dot(hcat, wc_ref[p], preferred_element_type=F32)
        y_ref[:, q * LANES:(q + 1) * LANES] = jax.nn.gelu(y)

    z = y_ref[...]
    gate = jnp.dot(z.astype(BF16), wglu_ref[...], preferred_element_type=F32) + bglu_ref[...]
    y_ref[...] = z * jax.nn.sigmoid(gate)
    for b in range(SCAN_BATCH):
        o_ref[b] = y_ref[pl.ds(_seq_row0(b, t_steps), t_steps), :].astype(o_ref.dtype)


def s5(u, wb, a_re, a_im, wc, d_skip, w_glu, b_glu, t_steps=None):
    bsz, seq, width = u.shape
    t_steps = t_steps or min(S5_T, seq)
    rows = _rows(t_steps)
    blk = pl.BlockSpec((SCAN_BATCH, t_steps, width), lambda g, t: (g, t, 0))
    nslab = 2 * S5_PAIRS
    return pl.pallas_call(
        functools.partial(_s5_kernel, t_steps=t_steps),
        grid=(bsz // SCAN_BATCH, seq // t_steps),
        in_specs=[blk, _const_spec(wb.shape), _const_spec(a_re.shape), _const_spec(a_im.shape),
                  _const_spec(wc.shape), _const_spec(d_skip.shape), _const_spec(w_glu.shape),
                  _const_spec(b_glu.shape)],
        out_specs=blk,
        out_shape=jax.ShapeDtypeStruct((bsz, seq, width), BF16),
        scratch_shapes=[pltpu.VMEM((rows, width), F32),
                        pltpu.VMEM((nslab, rows, LANES), F32),
                        pltpu.VMEM((nslab, SCAN_BATCH, LANES), F32),
                        pltpu.VMEM((rows, width), F32)],
        compiler_params=_cparams(("parallel", "arbitrary")),
        name="s5",
    )(u, wb, a_re, a_im, wc, d_skip, w_glu, b_glu)


def _s5_params(lam_re, lam_im, log_step, b_re, b_im, c_re, c_im):
    step = jnp.exp(log_step.astype(F32))[:, None]
    lr = jnp.minimum(lam_re.astype(F32), -1e-4)
    li = lam_im.astype(F32)
    mag = jnp.exp(lr * step)
    ang = li * step
    ab_re, ab_im = mag * jnp.cos(ang), mag * jnp.sin(ang)
    den = lr * lr + li * li
    nr, ni = ab_re - 1.0, ab_im
    f_re = (nr * lr + ni * li) / den
    f_im = (ni * lr - nr * li) / den
    bb_re = f_re[:, :, None] * b_re - f_im[:, :, None] * b_im
    bb_im = f_re[:, :, None] * b_im + f_im[:, :, None] * b_re
    g, n, c = bb_re.shape
    eye2 = jnp.eye(2, dtype=F32)
    eye_b = jnp.eye(S5_PAIRS_PER_BLOCK, dtype=F32)

    def in_mat(bb):
        bt = bb.transpose(0, 2, 1).reshape(S5_PAIRS, 2, c, n)
        m = jnp.einsum('pgcn,gh->pgchn', bt, eye2).reshape(S5_PAIRS, 2 * c, 2 * n)
        slot = jnp.arange(S5_PAIRS) % S5_PAIRS_PER_BLOCK
        sel = eye_b[slot]
        return jnp.einsum('pkn,ps->pskn', m, sel).reshape(S5_PAIRS, LANES, 2 * n)

    wb = jnp.concatenate([in_mat(bb_re), in_mat(bb_im)], axis=-1).astype(BF16)

    pairs_per_out = S5_PAIRS // (HALF_MIX // LANES)
    eye_o = jnp.eye(pairs_per_out, dtype=F32)

    def out_mat(cc):
        ct = cc.transpose(0, 2, 1).reshape(S5_PAIRS, 2, n, c)
        m = jnp.einsum('pgnc,gh->pgnhc', ct, eye2).reshape(S5_PAIRS, 2 * n, 2 * c)
        slot = jnp.arange(S5_PAIRS) % pairs_per_out
        sel = eye_o[slot]
        return jnp.einsum('pnk,ps->pnsk', m, sel).reshape(S5_PAIRS, 2 * n, LANES)

    wc = jnp.concatenate([out_mat(c_re.astype(F32)), -out_mat(c_im.astype(F32))],
                         axis=1).astype(BF16)
    a_re = ab_re.reshape(S5_PAIRS, 2 * n)
    a_im = ab_im.reshape(S5_PAIRS, 2 * n)
    return wb, a_re, a_im, wc


def _retention_kernel(q_ref, k_ref, v_ref, g_ref, ng_ref, din_ref, xi_ref, zeta_ref, mask_ref,
                      cd_ref, o_ref):
    seq = q_ref.shape[0]
    c = RET_CHUNK
    nc = seq // c
    r = None
    for ci in range(nc):
        rows = slice(ci * c, (ci + 1) * c)
        qb = q_ref[rows, :]
        kf = k_ref[rows, :].astype(F32)
        vb = v_ref[rows, :]
        inner = lax.dot_general(qb, (kf * mask_ref[...]).astype(BF16), (((1,), (1,)), ((), ())),
                                preferred_element_type=F32) * din_ref[...]
        o = jnp.dot(inner.astype(BF16), vb, preferred_element_type=F32)
        if r is not None:
            o = o + jnp.dot(qb, r.astype(BF16), preferred_element_type=F32) * xi_ref[...]
        if ci + 1 < nc:
            kz = (kf * zeta_ref[...]).astype(BF16)
            upd = lax.dot_general(kz, vb, (((0,), (0,)), ((), ())), preferred_element_type=F32)
            r = upd if r is None else cd_ref[...] * r + upd
        mu = jnp.mean(o, axis=-1, keepdims=True)
        oc = o - mu
        var = jnp.mean(oc * oc, axis=-1, keepdims=True)
        on = oc * lax.rsqrt(var + LN_EPS) * ng_ref[...]
        gt = g_ref[rows, :]
        o_ref[rows, :] = (gt * jax.nn.sigmoid(gt) * on).astype(o_ref.dtype)


def retention(q, k, v, g, norm_g, tables):
    bsz, seq, _ = q.shape
    width = v.shape[2]
    heads = width // LANES
    heads_per_blk = LANES // RET_QK
    qk_spec = pl.BlockSpec((None, seq, LANES), lambda b, h: (b, 0, h // heads_per_blk))
    v_spec = pl.BlockSpec((None, seq, LANES), lambda b, h: (b, 0, h))
    tab_spec = pl.BlockSpec((None, RET_CHUNK, LANES), lambda b, h: (h, 0, 0))
    cd_spec = pl.BlockSpec((None, 1, LANES), lambda b, h: (h, 0, 0))
    return pl.pallas_call(
        _retention_kernel,
        grid=(bsz, heads),
        in_specs=[qk_spec, qk_spec, v_spec, v_spec, _const_spec(norm_g.shape),
                  tab_spec, tab_spec, tab_spec, cd_spec, cd_spec],
        out_specs=v_spec,
        out_shape=jax.ShapeDtypeStruct((bsz, seq, width), BF16),
        compiler_params=_cparams(("parallel", "parallel")),
        name="retention",
    )(q, k, v, g, norm_g, *tables)


def _retention_tables():
    c = RET_CHUNK
    log_g = jnp.log(1.0 - jnp.exp2(-5.0 - jnp.arange(RET_HEADS, dtype=F32)))
    idx = jnp.arange(c, dtype=F32)
    diff = idx[:, None] - idx[None, :]
    causal = diff >= 0
    decay_in = jnp.where(causal[None], jnp.exp(log_g[:, None, None] * jnp.where(causal, diff, 0.0)[None]), 0.0)
    xi = jnp.exp(log_g[:, None] * (idx + 1.0))
    zeta = jnp.exp(log_g[:, None] * (c - 1.0 - idx))
    chunk_decay = jnp.exp(log_g * c)
    bc = lambda t: jnp.broadcast_to(t[:, :, None], (RET_HEADS, c, LANES))
    cd = jnp.broadcast_to(chunk_decay[:, None, None], (RET_HEADS, 1, LANES))
    lane_head = jnp.arange(LANES) // RET_QK
    mask = (lane_head[None, :] == (jnp.arange(RET_HEADS) % (LANES // RET_QK))[:, None]).astype(F32)
    mask = mask[:, None, :]
    return decay_in, bc(xi), bc(zeta) * mask, mask, cd


def _kv_proj_kernel(m_ref, w_ref, k_ref, v_ref):
    mb = m_ref[...].astype(BF16)
    d = k_ref.shape[1]
    k_ref[...] = jnp.dot(mb, w_ref[:, :d], preferred_element_type=F32).astype(k_ref.dtype)
    v_ref[...] = jnp.dot(mb, w_ref[:, d:], preferred_element_type=F32).astype(v_ref.dtype)


def kv_proj(mem, w):
    n, d = mem.shape
    return pl.pallas_call(
        _kv_proj_kernel,
        grid=(n // TM,),
        in_specs=[pl.BlockSpec((TM, d), lambda i: (i, 0)), _const_spec(w.shape)],
        out_specs=[pl.BlockSpec((TM, d), lambda i: (i, 0))] * 2,
        out_shape=[jax.ShapeDtypeStruct((n, d), BF16)] * 2,
        compiler_params=_cparams(("parallel",)),
        name="kv_proj",
    )(mem, w)


def _xattn_kernel(x_ref, k_ref, v_ref, wq_ref, wo_ref, g_ref, b_ref, o_ref, oh_ref):
    half = x_ref.shape[0] // 2
    for r in range(2):
        rows = slice(r * half, (r + 1) * half)
        x = x_ref[rows, :]
        q = (jnp.dot(x.astype(BF16), wq_ref[...], preferred_element_type=F32)
             * (XA_HEAD ** -0.5)).astype(BF16)
        for h in range(XA_HEADS):
            cols = slice(h * XA_HEAD, (h + 1) * XA_HEAD)
            s = lax.dot_general(q[:, cols], k_ref[:, cols], (((1,), (1,)), ((), ())),
                                preferred_element_type=F32)
            s = s - jnp.max(s, axis=-1, keepdims=True)
            p = jnp.exp(s)
            p = p / jnp.sum(p, axis=-1, keepdims=True)
            oh_ref[rows, cols] = jnp.dot(p.astype(BF16), v_ref[:, cols],
                                         preferred_element_type=F32).astype(BF16)
        c = jnp.dot(oh_ref[rows, :], wo_ref[...], preferred_element_type=F32)
        o_ref[rows, :] = _layer_norm(DN_ALPHA * x + c, g_ref[...], b_ref[...])


def xattn_ln(x, k, v, wq, wo, g, b, seq):
    n, d = x.shape
    tm = min(TM_LN, seq)
    tiles_per_seq = seq // tm
    kv_spec = pl.BlockSpec((MEM_LEN, d), lambda i: (i // tiles_per_seq, 0))
    return pl.pallas_call(
        _xattn_kernel,
        grid=(n // tm,),
        in_specs=[pl.BlockSpec((tm, d), lambda i: (i, 0)), kv_spec, kv_spec,
                  _const_spec(wq.shape), _const_spec(wo.shape),
                  _const_spec((1, d)), _const_spec((1, d))],
        out_specs=pl.BlockSpec((tm, d), lambda i: (i, 0)),
        out_shape=jax.ShapeDtypeStruct((n, d), F32),
        scratch_shapes=[pltpu.VMEM((tm, d), BF16)],
        compiler_params=_cparams(("parallel",)),
        name="xattn_ln",
    )(x, k, v, wq, wo, g, b)


def _ffn_weights(w_gate, w_up, w_down):
    return w_gate.astype(BF16), w_up.astype(BF16), w_down.astype(BF16)


def _lru_gate_matrix(gate_w):
    eye = jnp.eye(LRU_BLOCKS, dtype=F32)
    dense = jnp.einsum('gncd,nm->gncmd', gate_w.astype(F32), eye)
    width = LRU_BLOCKS * LRU_BLOCK
    dense = dense.reshape(2, width, width)
    return jnp.concatenate([dense[0], dense[1]], axis=-1).astype(BF16)


def _diff_lambda_init(layer):
    return 0.8 - 0.6 * math.exp(-0.3 * layer)


EVEN_SECTIONS = (
    (0, HALF_MIX, True, DA_HEAD ** -0.5 * math.log2(math.e)),
    (HALF_MIX, HALF_MIX, True, 1.0),
    (2 * HALF_MIX, HALF_MIX, False, 1.0),
    (3 * HALF_MIX, HALF_MIX, False, 1.0),
    (4 * HALF_MIX, HALF_MIX, False, 1.0),
)
EVEN_DTYPES = (BF16, BF16, BF16, F32, F32)
RET_QK_W = RET_HEADS * RET_QK
ODD_SECTIONS = (
    (0, HALF_MIX, False, 1.0),
    (HALF_MIX, RET_QK_W, True, 1.0),
    (HALF_MIX + RET_QK_W, RET_QK_W, True, RET_QK ** -0.5),
    (HALF_MIX + 2 * RET_QK_W, HALF_MIX, False, 1.0),
    (2 * HALF_MIX + 2 * RET_QK_W, HALF_MIX, False, 1.0),
)
ODD_DTYPES = (F32, BF16, BF16, BF16, F32)


def kernel(x, mem, ln_g, ln_b, ffn_w_gate, ffn_w_up, ffn_w_down, xa_w_q, xa_w_kv, xa_w_o,
           ev_w_in, ev_w_out, da_lambda, da_norm_g, lru_conv_w, lru_conv_b, lru_gate_w,
           lru_gate_b, lru_lambda, od_w_in, od_w_out, s5_lam_re, s5_lam_im, s5_log_step,
           s5_b_re, s5_b_im, s5_c_re, s5_c_im, s5_d, s5_glu_w, s5_glu_b, ret_norm_g):
    bsz, seq, d = x.shape
    n = bsz * seq
    depth = ln_g.shape[0]
    mem2 = mem.reshape(bsz * mem.shape[1], d)
    da_tables = _rotary_tables(seq, DA_ROPE, ROPE_THETA, DA_HEAD)
    ret_rot_tables = _rotary_tables(seq, RET_QK, RET_THETA, RET_QK)
    ret_tables = _retention_tables()
    row = lambda v: v.reshape(1, -1).astype(F32)

    h = x.reshape(n, d)
    for l in range(depth):
        wg, wu, wd = _ffn_weights(ffn_w_gate[l, 0], ffn_w_up[l, 0], ffn_w_down[l, 0])
        h = ffn_ln(h, wg, wu, wd, row(ln_g[l, 0]), row(ln_b[l, 0]))

        if l % 2 == 0:
            e = l // 2
            q, k, v, gate, xr = in_proj(h, ev_w_in[e].astype(BF16), da_tables, EVEN_SECTIONS,
                                        EVEN_DTYPES, DA_ROPE // 2, seq)
            shp = (bsz, seq, HALF_MIX)
            a_out = diff_attn(q.reshape(shp), k.reshape(shp), v.reshape(shp), da_lambda[e],
                              row(da_norm_g[e]), _diff_lambda_init(l))
            b_out = rglru(xr.reshape(shp), gate.reshape(shp), lru_conv_w[e], row(lru_conv_b[e]),
                          _lru_gate_matrix(lru_gate_w[e]), row(lru_gate_b[e]), row(lru_lambda[e]))
            w_out = ev_w_out[e]
        else:
            o = l // 2
            u, q, k, v, g = in_proj(h, od_w_in[o].astype(BF16), ret_rot_tables, ODD_SECTIONS,
                                    ODD_DTYPES, RET_QK // 2, seq)
            shp = (bsz, seq, HALF_MIX)
            wb, a_re, a_im, wc = _s5_params(s5_lam_re[o], s5_lam_im[o], s5_log_step[o],
                                            s5_b_re[o], s5_b_im[o], s5_c_re[o], s5_c_im[o])
            a_out = s5(u.reshape(shp), wb, a_re, a_im, wc, row(s5_d[o]),
                       s5_glu_w[o].astype(BF16), row(s5_glu_b[o]))
            qk_shp = (bsz, seq, RET_QK_W)
            b_out = retention(q.reshape(qk_shp), k.reshape(qk_shp), v.reshape(shp),
                              g.reshape(shp), row(ret_norm_g[o]), ret_tables)
            w_out = od_w_out[o]
        h = out_proj_ln(h, a_out.reshape(n, HALF_MIX), b_out.reshape(n, HALF_MIX),
                        w_out.astype(BF16), row(ln_g[l, 1]), row(ln_b[l, 1]))

        kk, vv = kv_proj(mem2, xa_w_kv[l].astype(BF16))
        h = xattn_ln(h, kk, vv, xa_w_q[l].astype(BF16), xa_w_o[l].astype(BF16),
                     row(ln_g[l, 2]), row(ln_b[l, 2]), seq)

        wg, wu, wd = _ffn_weights(ffn_w_gate[l, 1], ffn_w_up[l, 1], ffn_w_down[l, 1])
        h = ffn_ln(h, wg, wu, wd, row(ln_g[l, 3]), row(ln_b[l, 3]))
    return h.reshape(bsz, seq, d)
```

```python
import functools
import math

import jax
import jax.numpy as jnp
from jax import lax
from jax.experimental import pallas as pl
from jax.experimental.pallas import tpu as pltpu

F32 = jnp.float32
BF16 = jnp.bfloat16

D_MODEL = 1024
DEPTH = 4
HALF_MIX = 512
DA_HEADS = 4
DA_HEAD = 64
DA_ROPE = 16
ROPE_THETA = 500000.0
LRU_BLOCKS = 8
LRU_BLOCK = 64
CONV_W = 4
LRU_C = 8.0
S5_GROUP = 16
S5_GROUPS = 32
S5_STATE = 64
RET_HEADS = 4
RET_QK = 64
RET_V = 128
RET_THETA = 10000.0
RET_CHUNK = 128
MEM_LEN = 256
XA_HEADS = 4
XA_HEAD = 256
D_FF = 2816
DN_ALPHA = (2 * DEPTH) ** 0.25
LN_EPS = 1e-5

LANES = 128
SUBLANES = 8
VMEM_LIMIT = 56 * 1024 * 1024

TM = 512
TM_LN = 1024
FF_CHUNK = 256
TQ = 256
SCAN_BATCH = SUBLANES
LRU_T = 256
S5_T = 128
PITCH_PAD = 4
NEG_BIG = -1e30


def _cparams(sem):
    return pltpu.CompilerParams(dimension_semantics=sem, vmem_limit_bytes=VMEM_LIMIT)


def _const_spec(shape):
    nd = len(shape)
    return pl.BlockSpec(shape, lambda *_: (0,) * nd, pipeline_mode=pl.Buffered(1))


def _stacked_spec(arr, idx):
    tail = arr.shape[len(idx):]
    return pl.BlockSpec((None,) * len(idx) + tail, lambda *_: tuple(idx) + (0,) * len(tail),
                        pipeline_mode=pl.Buffered(1))


def _emit_skewed(first, second):
    first[0]()
    for i in range(1, len(first)):
        second[i - 1]()
        first[i]()
    second[-1]()


def _layer_norm(y, g, b):
    mu = jnp.mean(y, axis=-1, keepdims=True)
    yc = y - mu
    var = jnp.mean(yc * yc, axis=-1, keepdims=True)
    return yc * lax.rsqrt(var + LN_EPS) * g + b


def _ffn_kernel(x_ref, wg_ref, wu_ref, wd_ref, g_ref, b_ref, o_ref, xb_ref, a_ref):
    xb_ref[...] = x_ref[...].astype(BF16)
    dff = wg_ref.shape[1]
    for c0 in range(0, dff, FF_CHUNK):
        cols = slice(c0, c0 + FF_CHUNK)
        xb = xb_ref[...]
        g = jnp.dot(xb, wg_ref[:, cols], preferred_element_type=F32)
        u = jnp.dot(xb, wu_ref[:, cols], preferred_element_type=F32)
        a_ref[:, cols] = (g * jax.nn.sigmoid(g) * u).astype(BF16)
    half = x_ref.shape[0] // 2
    for r in range(2):
        rows = slice(r * half, (r + 1) * half)
        acc = jnp.dot(a_ref[rows, :], wd_ref[...], preferred_element_type=F32)
        y = DN_ALPHA * x_ref[rows, :] + 0.5 * acc
        o_ref[rows, :] = _layer_norm(y, g_ref[...], b_ref[...])


def ffn_ln(x, wg, wu, wd, idx, g, b):
    n, d = x.shape
    dff = wg.shape[-1]
    tm = min(TM_LN, n)
    return pl.pallas_call(
        _ffn_kernel,
        grid=(n // tm,),
        in_specs=[
            pl.BlockSpec((tm, d), lambda i: (i, 0)),
            _stacked_spec(wg, idx),
            _stacked_spec(wu, idx),
            _stacked_spec(wd, idx),
            _const_spec((1, d)),
            _const_spec((1, d)),
        ],
        out_specs=pl.BlockSpec((tm, d), lambda i: (i, 0)),
        out_shape=jax.ShapeDtypeStruct((n, d), F32),
        scratch_shapes=[pltpu.VMEM((tm, d), BF16), pltpu.VMEM((tm, dff), BF16)],
        compiler_params=_cparams(("parallel",)),
        name="ffn_ln",
    )(x, wg, wu, wd, g, b)


def _in_proj_kernel(x_ref, w_ref, cos_ref, sa_ref, sb_ref, *o_refs, sections, rot_shift):
    xb = x_ref[...].astype(BF16)
    for (start, width, rotate, scale), o_ref in zip(sections, o_refs):
        z = jnp.dot(xb, w_ref[:, start:start + width], preferred_element_type=F32)
        if rotate:
            cos, sa, sb = cos_ref[...], sa_ref[...], sb_ref[...]
            for j in range(width // LANES):
                zj = z[:, j * LANES:(j + 1) * LANES]
                zr = (zj * cos + pltpu.roll(zj, LANES - rot_shift, 1) * sa
                      + pltpu.roll(zj, rot_shift, 1) * sb)
                o_ref[:, j * LANES:(j + 1) * LANES] = (zr * scale).astype(o_ref.dtype)
        else:
            o_ref[...] = (z * scale).astype(o_ref.dtype)


def in_proj(x, w, tables, sections, out_dtypes, rot_shift, seq):
    n, d = x.shape
    tiles_per_seq = seq // TM
    tab_spec = pl.BlockSpec((TM, LANES), lambda i: (i % tiles_per_seq, 0))
    return pl.pallas_call(
        functools.partial(_in_proj_kernel, sections=sections, rot_shift=rot_shift),
        grid=(n // TM,),
        in_specs=[pl.BlockSpec((TM, d), lambda i: (i, 0)), _const_spec(w.shape),
                  tab_spec, tab_spec, tab_spec],
        out_specs=[pl.BlockSpec((TM, s[1]), lambda i: (i, 0)) for s in sections],
        out_shape=[jax.ShapeDtypeStruct((n, s[1]), dt) for s, dt in zip(sections, out_dtypes)],
        compiler_params=_cparams(("parallel",)),
        name="in_proj",
    )(x, w, *tables)


def _rotary_tables(seq, rot_dim, theta, group):
    half = rot_dim // 2
    inv = theta ** (-jnp.arange(half, dtype=F32) * 2.0 / rot_dim)
    ang = jnp.arange(seq, dtype=jnp.int32).astype(F32)[:, None] * inv[None, :]
    cos, sin = jnp.cos(ang), jnp.sin(ang)
    zeros = jnp.zeros((seq, group - rot_dim), F32)
    zh = jnp.zeros((seq, half), F32)
    cos_g = jnp.concatenate([cos, cos, jnp.ones((seq, group - rot_dim), F32)], -1)
    sa_g = jnp.concatenate([-sin, zh, zeros], -1)
    sb_g = jnp.concatenate([zh, sin, zeros], -1)
    reps = LANES // group
    return tuple(jnp.tile(t, (1, reps)) for t in (cos_g, sa_g, sb_g))


def _diff_attn_kernel(q_ref, k_ref, v_ref, lam_ref, g_ref, o_ref, vx_ref, s0_ref, s1_ref,
                      w0_ref, w1_ref, *, lambda_init):
    seq = q_ref.shape[0]
    tq = TQ
    vx_ref[:, :LANES] = v_ref[...]
    vx_ref[:, LANES:] = jnp.ones((seq, LANES), BF16)
    lam = lam_ref[...]
    lmbda = (jnp.exp(jnp.sum(lam[0:1] * lam[1:2], axis=-1, keepdims=True))
             - jnp.exp(jnp.sum(lam[2:3] * lam[3:4], axis=-1, keepdims=True)) + lambda_init)
    lane = lax.broadcasted_iota(jnp.int32, (tq, LANES), 1)
    row = lax.broadcasted_iota(jnp.int32, (2 * tq, tq), 0)
    col = lax.broadcasted_iota(jnp.int32, (2 * tq, tq), 1)
    causal = col <= jnp.where(row >= tq, row - tq, row)

    s_bufs, w_bufs = (s0_ref, s1_ref), (w0_ref, w1_ref)

    def scores(qi, s_buf):
        qf = q_ref[qi * tq:(qi + 1) * tq, :].astype(F32)
        qq = jnp.concatenate([jnp.where(lane < DA_HEAD, qf, 0.0),
                              jnp.where(lane >= DA_HEAD, qf, 0.0)], axis=0).astype(BF16)
        m_run = None
        for j in range(qi + 1):
            s = lax.dot_general(qq, k_ref[j * tq:(j + 1) * tq, :], (((1,), (1,)), ((), ())),
                                preferred_element_type=F32)
            if j == qi:
                s = jnp.where(causal, s, NEG_BIG)
            s_buf[:, j * tq:(j + 1) * tq] = s
            for c in range(tq // LANES):
                sc = s[:, c * LANES:(c + 1) * LANES]
                m_run = sc if m_run is None else jnp.maximum(m_run, sc)
        return jnp.max(m_run, axis=-1, keepdims=True)

    def finish(qi, m, s_buf, p_buf):
        for j in range(qi + 1):
            cols = slice(j * tq, (j + 1) * tq)
            p_buf[:, cols] = jnp.exp2(s_buf[:, cols] - m).astype(BF16)
        kv = (qi + 1) * tq
        pv = jnp.dot(p_buf[:, :kv], vx_ref[:kv, :], preferred_element_type=F32)
        o = pv[:, :LANES] / pv[:, LANES:]
        d = o[:tq] - lmbda * o[tq:]
        rms = lax.rsqrt(jnp.mean(d * d, axis=-1, keepdims=True) + LN_EPS)
        o_ref[qi * tq:(qi + 1) * tq, :] = (
            d * rms * g_ref[...] * (1.0 - lambda_init)).astype(o_ref.dtype)

    order = list(reversed(range(seq // tq)))
    pending = None
    for n, qi in enumerate(order):
        m = scores(qi, s_bufs[n % 2])
        if pending is not None:
            finish(*pending)
        pending = (qi, m, s_bufs[n % 2], w_bufs[n % 2])
    finish(*pending)


def diff_attn(q, k, v, lam, norm_g, lambda_init):
    bsz, seq, width = q.shape
    heads = width // LANES
    spec = pl.BlockSpec((None, seq, LANES), lambda b, h: (b, 0, h))
    return pl.pallas_call(
        functools.partial(_diff_attn_kernel, lambda_init=lambda_init),
        grid=(bsz, heads),
        in_specs=[spec, spec, spec, _const_spec(lam.shape), _const_spec(norm_g.shape)],
        out_specs=spec,
        out_shape=jax.ShapeDtypeStruct((bsz, seq, width), BF16),
        scratch_shapes=[pltpu.VMEM((seq, 2 * LANES), BF16),
                        pltpu.VMEM((2 * TQ, seq), F32), pltpu.VMEM((2 * TQ, seq), F32),
                        pltpu.VMEM((2 * TQ, seq), BF16), pltpu.VMEM((2 * TQ, seq), BF16)],
        compiler_params=_cparams(("parallel", "parallel")),
        name="diff_attn",
    )(q, k, v, lam, norm_g)


def _pitch(t_steps):
    return t_steps + PITCH_PAD


def _rows(t_steps):
    return SCAN_BATCH * _pitch(t_steps)


def _seq_row0(b, t_steps):
    return b * _pitch(t_steps) + PITCH_PAD


def _time_rows(t, t_steps):
    return pl.ds(PITCH_PAD + t, SCAN_BATCH, stride=_pitch(t_steps))


CONV_HEAD = SUBLANES


def _rglru_kernel(xr_ref, gate_ref, cw_ref, cb_ref, wg_ref, gb_ref, lam_ref, o_ref,
                  xpad_ref, a_ref, b_ref, h_ref, *, t_steps, row_chunks):
    ti = pl.program_id(1)
    pitch, rows = _pitch(t_steps), _rows(t_steps)
    width = xr_ref.shape[2]
    nslab = width // LANES
    hist = CONV_W - 1

    @pl.when(ti == 0)
    def _():
        xpad_ref[...] = jnp.zeros_like(xpad_ref)
        h_ref[...] = jnp.zeros_like(h_ref)

    @pl.when(ti > 0)
    def _():
        for b in range(SCAN_BATCH):
            r0 = CONV_HEAD + _seq_row0(b, t_steps)
            xpad_ref[pl.ds(r0 - hist, hist), :] = xpad_ref[pl.ds(r0 + t_steps - hist, hist), :]

    for b in range(SCAN_BATCH):
        xpad_ref[pl.ds(CONV_HEAD + _seq_row0(b, t_steps), t_steps), :] = xr_ref[b]

    softplus = jax.nn.softplus(-lam_ref[...])
    chunk = rows // row_chunks
    for c in range(row_chunks):
        r0 = c * chunk
        xc = cb_ref[...] + cw_ref[0:1, :] * xpad_ref[pl.ds(CONV_HEAD + r0 - hist, chunk), :]
        for j in range(1, CONV_W):
            xc = xc + cw_ref[j:j + 1, :] * xpad_ref[pl.ds(CONV_HEAD + r0 - hist + j, chunk), :]
        gates = jnp.dot(xc.astype(BF16), wg_ref[...], preferred_element_type=F32) + gb_ref[...]
        r = jax.nn.sigmoid(gates[:, :width])
        i = jax.nn.sigmoid(gates[:, width:])
        log_a = -LRU_C * r * softplus
        a = jnp.exp(log_a)
        bb = jnp.sqrt(-jnp.tanh(log_a) * (a * a + 1.0)) * (i * xc)
        for s in range(nslab):
            a_ref[s, pl.ds(r0, chunk), :] = a[:, s * LANES:(s + 1) * LANES]
            b_ref[s, pl.ds(r0, chunk), :] = bb[:, s * LANES:(s + 1) * LANES]

    def step(t, hs):
        new = []
        for s in range(nslab):
            rows_t = _time_rows(t, t_steps)
            h = a_ref[s, rows_t, :] * hs[s] + b_ref[s, rows_t, :]
            b_ref[s, rows_t, :] = h
            new.append(h)
        return tuple(new)

    hs = lax.fori_loop(0, t_steps, step, tuple(h_ref[s] for s in range(nslab)), unroll=8)
    for s in range(nslab):
        h_ref[s] = hs[s]

    for b in range(SCAN_BATCH):
        r0 = _seq_row0(b, t_steps)
        for s in range(nslab):
            h = b_ref[s, pl.ds(r0, t_steps), :]
            gt = gate_ref[b, :, s * LANES:(s + 1) * LANES]
            o_ref[b, :, s * LANES:(s + 1) * LANES] = (jax.nn.gelu(gt) * h).astype(o_ref.dtype)


def rglru(xr, gate, conv_w, conv_b, w_gates, gate_b, lru_lam, t_steps=None):
    bsz, seq, width = xr.shape
    t_steps = t_steps or min(LRU_T, seq)
    rows = _rows(t_steps)
    row_chunks = next(c for c in (5, 4, 3, 2, 1) if (rows // SUBLANES) % c == 0)
    blk = pl.BlockSpec((SCAN_BATCH, t_steps, width), lambda g, t: (g, t, 0))
    return pl.pallas_call(
        functools.partial(_rglru_kernel, t_steps=t_steps, row_chunks=row_chunks),
        grid=(bsz // SCAN_BATCH, seq // t_steps),
        in_specs=[blk, blk, _const_spec(conv_w.shape), _const_spec(conv_b.shape),
                  _const_spec(w_gates.shape), _const_spec(gate_b.shape),
                  _const_spec(lru_lam.shape)],
        out_specs=blk,
        out_shape=jax.ShapeDtypeStruct((bsz, seq, width), BF16),
        scratch_shapes=[pltpu.VMEM((CONV_HEAD + rows, width), F32),
                        pltpu.VMEM((width // LANES, rows, LANES), F32),
                        pltpu.VMEM((width // LANES, rows, LANES), F32),
                        pltpu.VMEM((width // LANES, SCAN_BATCH, LANES), F32)],
        compiler_params=_cparams(("parallel", "arbitrary")),
        name="rglru",
    )(xr, gate, conv_w, conv_b, w_gates, gate_b, lru_lam)


S5_PAIRS = S5_GROUPS // 2
S5_PAIRS_PER_PASS = 4
S5_PAIRS_PER_BLOCK = LANES // (2 * S5_GROUP)


def _s5_kernel(u_ref, wb_ref, are_ref, aim_ref, wc_ref, d_ref, wglu_ref, bglu_ref, o_ref,
               upad_ref, xs_ref, h_ref, y_ref, *, t_steps):
    ti = pl.program_id(1)
    rows = _rows(t_steps)
    width = u_ref.shape[2]

    @pl.when(ti == 0)
    def _():
        upad_ref[...] = jnp.zeros_like(upad_ref)
        h_ref[...] = jnp.zeros_like(h_ref)

    for b in range(SCAN_BATCH):
        upad_ref[pl.ds(_seq_row0(b, t_steps), t_steps), :] = u_ref[b]

    for p in range(S5_PAIRS):
        blk = p // S5_PAIRS_PER_BLOCK
        ub = upad_ref[:, blk * LANES:(blk + 1) * LANES].astype(BF16)
        x = jnp.dot(ub, wb_ref[p], preferred_element_type=F32)
        xs_ref[2 * p] = x[:, :LANES]
        xs_ref[2 * p + 1] = x[:, LANES:]

    for p0 in range(0, S5_PAIRS, S5_PAIRS_PER_PASS):
        pairs = range(p0, p0 + S5_PAIRS_PER_PASS)
        a_re = [jnp.broadcast_to(are_ref[p:p + 1, :], (SCAN_BATCH, LANES)) for p in pairs]
        a_im = [jnp.broadcast_to(aim_ref[p:p + 1, :], (SCAN_BATCH, LANES)) for p in pairs]

        def step(t, hs, pairs=pairs, a_re=a_re, a_im=a_im):
            new = []
            rows_t = _time_rows(t, t_steps)
            for n, p in enumerate(pairs):
                h_re, h_im = hs[2 * n], hs[2 * n + 1]
                n_re = a_re[n] * h_re - a_im[n] * h_im + xs_ref[2 * p, rows_t, :]
                n_im = a_re[n] * h_im + a_im[n] * h_re + xs_ref[2 * p + 1, rows_t, :]
                xs_ref[2 * p, rows_t, :] = n_re
                xs_ref[2 * p + 1, rows_t, :] = n_im
                new += [n_re, n_im]
            return tuple(new)

        init = tuple(h_ref[s] for p in pairs for s in (2 * p, 2 * p + 1))
        hs = lax.fori_loop(0, t_steps, step, init, unroll=4)
        for n, p in enumerate(pairs):
            h_ref[2 * p] = hs[2 * n]
            h_ref[2 * p + 1] = hs[2 * n + 1]

    pairs_per_out = S5_PAIRS // (width // LANES)
    for q in range(width // LANES):
        y = d_ref[:, q * LANES:(q + 1) * LANES] * upad_ref[:, q * LANES:(q + 1) * LANES]
        for p in range(q * pairs_per_out, (q + 1) * pairs_per_out):
            hcat = jnp.concatenate([xs_ref[2 * p], xs_ref[2 * p + 1]], axis=-1).astype(BF16)
            y = y + jnp.dot(hcat, wc_ref[p], preferred_element_type=F32)
        y_ref[:, q * LANES:(q + 1) * LANES] = jax.nn.gelu(y)

    z = y_ref[...]
    gate = jnp.dot(z.astype(BF16), wglu_ref[...], preferred_element_type=F32) + bglu_ref[...]
    y_ref[...] = z * jax.nn.sigmoid(gate)
    for b in range(SCAN_BATCH):
        o_ref[b] = y_ref[pl.ds(_seq_row0(b, t_steps), t_steps), :].astype(o_ref.dtype)


def s5(u, wb, a_re, a_im, wc, d_skip, w_glu, b_glu, t_steps=None):
    bsz, seq, width = u.shape
    t_steps = t_steps or min(S5_T, seq)
    rows = _rows(t_steps)
    blk = pl.BlockSpec((SCAN_BATCH, t_steps, width), lambda g, t: (g, t, 0))
    nslab = 2 * S5_PAIRS
    return pl.pallas_call(
        functools.partial(_s5_kernel, t_steps=t_steps),
        grid=(bsz // SCAN_BATCH, seq // t_steps),
        in_specs=[blk, _const_spec(wb.shape), _const_spec(a_re.shape), _const_spec(a_im.shape),
                  _const_spec(wc.shape), _const_spec(d_skip.shape), _const_spec(w_glu.shape),
                  _const_spec(b_glu.shape)],
        out_specs=blk,
        out_shape=jax.ShapeDtypeStruct((bsz, seq, width), BF16),
        scratch_shapes=[pltpu.VMEM((rows, width), F32),
                        pltpu.VMEM((nslab, rows, LANES), F32),
                        pltpu.VMEM((nslab, SCAN_BATCH, LANES), F32),
                        pltpu.VMEM((rows, width), F32)],
        compiler_params=_cparams(("parallel", "arbitrary")),
        name="s5",
    )(u, wb, a_re, a_im, wc, d_skip, w_glu, b_glu)


def _s5_params(lam_re, lam_im, log_step, b_re, b_im, c_re, c_im):
    step = jnp.exp(log_step.astype(F32))[:, None]
    lr = jnp.minimum(lam_re.astype(F32), -1e-4)
    li = lam_im.astype(F32)
    mag = jnp.exp(lr * step)
    ang = li * step
    ab_re, ab_im = mag * jnp.cos(ang), mag * jnp.sin(ang)
    den = lr * lr + li * li
    nr, ni = ab_re - 1.0, ab_im
    f_re = (nr * lr + ni * li) / den
    f_im = (ni * lr - nr * li) / den
    bb_re = f_re[:, :, None] * b_re - f_im[:, :, None] * b_im
    bb_im = f_re[:, :, None] * b_im + f_im[:, :, None] * b_re
    g, n, c = bb_re.shape
    eye2 = jnp.eye(2, dtype=F32)
    eye_b = jnp.eye(S5_PAIRS_PER_BLOCK, dtype=F32)

    def in_mat(bb):
        bt = bb.transpose(0, 2, 1).reshape(S5_PAIRS, 2, c, n)
        m = jnp.einsum('pgcn,gh->pgchn', bt, eye2).reshape(S5_PAIRS, 2 * c, 2 * n)
        slot = jnp.arange(S5_PAIRS) % S5_PAIRS_PER_BLOCK
        sel = eye_b[slot]
        return jnp.einsum('pkn,ps->pskn', m, sel).reshape(S5_PAIRS, LANES, 2 * n)

    wb = jnp.concatenate([in_mat(bb_re), in_mat(bb_im)], axis=-1).astype(BF16)

    pairs_per_out = S5_PAIRS // (HALF_MIX // LANES)
    eye_o = jnp.eye(pairs_per_out, dtype=F32)

    def out_mat(cc):
        ct = cc.transpose(0, 2, 1).reshape(S5_PAIRS, 2, n, c)
        m = jnp.einsum('pgnc,gh->pgnhc', ct, eye2).reshape(S5_PAIRS, 2 * n, 2 * c)
        slot = jnp.arange(S5_PAIRS) % pairs_per_out
        sel = eye_o[slot]
        return jnp.einsum('pnk,ps->pnsk', m, sel).reshape(S5_PAIRS, 2 * n, LANES)

    wc = jnp.concatenate([out_mat(c_re.astype(F32)), -out_mat(c_im.astype(F32))],
                         axis=1).astype(BF16)
    a_re = ab_re.reshape(S5_PAIRS, 2 * n)
    a_im = ab_im.reshape(S5_PAIRS, 2 * n)
    return wb, a_re, a_im, wc


def _retention_kernel(q_ref, k_ref, v_ref, g_ref, ng_ref, din_ref, xi_ref, zeta_ref, mask_ref,
                      cd_ref, o_ref):
    seq = q_ref.shape[0]
    c = RET_CHUNK
    nc = seq // c
    r = None
    for ci in range(nc):
        rows = slice(ci * c, (ci + 1) * c)
        qb = q_ref[rows, :]
        kf = k_ref[rows, :].astype(F32)
        vb = v_ref[rows, :]
        inner = lax.dot_general(qb, (kf * mask_ref[...]).astype(BF16), (((1,), (1,)), ((), ())),
                                preferred_element_type=F32) * din_ref[...]
        o = jnp.dot(inner.astype(BF16), vb, preferred_element_type=F32)
        if r is not None:
            o = o + jnp.dot(qb, r.astype(BF16), preferred_element_type=F32) * xi_ref[...]
        if ci + 1 < nc:
            kz = (kf * zeta_ref[...]).astype(BF16)
            upd = lax.dot_general(kz, vb, (((0,), (0,)), ((), ())), preferred_element_type=F32)
            r = upd if r is None else cd_ref[...] * r + upd
        mu = jnp.mean(o, axis=-1, keepdims=True)
        oc = o - mu
        var = jnp.mean(oc * oc, axis=-1, keepdims=True)
        on = oc * lax.rsqrt(var + LN_EPS) * ng_ref[...]
        gt = g_ref[rows, :]
        o_ref[rows, :] = (gt * jax.nn.sigmoid(gt) * on).astype(o_ref.dtype)


def retention(q, k, v, g, norm_g, tables):
    bsz, seq, _ = q.shape
    width = v.shape[2]
    heads = width // LANES
    heads_per_blk = LANES // RET_QK
    qk_spec = pl.BlockSpec((None, seq, LANES), lambda b, h: (b, 0, h // heads_per_blk))
    v_spec = pl.BlockSpec((None, seq, LANES), lambda b, h: (b, 0, h))
    tab_spec = pl.BlockSpec((None, RET_CHUNK, LANES), lambda b, h: (h, 0, 0))
    cd_spec = pl.BlockSpec((None, 1, LANES), lambda b, h: (h, 0, 0))
    return pl.pallas_call(
        _retention_kernel,
        grid=(bsz, heads),
        in_specs=[qk_spec, qk_spec, v_spec, v_spec, _const_spec(norm_g.shape),
                  tab_spec, tab_spec, tab_spec, cd_spec, cd_spec],
        out_specs=v_spec,
        out_shape=jax.ShapeDtypeStruct((bsz, seq, width), BF16),
        compiler_params=_cparams(("parallel", "parallel")),
        name="retention",
    )(q, k, v, g, norm_g, *tables)


def _retention_tables():
    c = RET_CHUNK
    log_g = jnp.log(1.0 - jnp.exp2(-5.0 - jnp.arange(RET_HEADS, dtype=F32)))
    idx = jnp.arange(c, dtype=F32)
    diff = idx[:, None] - idx[None, :]
    causal = diff >= 0
    decay_in = jnp.where(causal[None], jnp.exp(log_g[:, None, None] * jnp.where(causal, diff, 0.0)[None]), 0.0)
    xi = jnp.exp(log_g[:, None] * (idx + 1.0))
    zeta = jnp.exp(log_g[:, None] * (c - 1.0 - idx))
    chunk_decay = jnp.exp(log_g * c)
    bc = lambda t: jnp.broadcast_to(t[:, :, None], (RET_HEADS, c, LANES))
    cd = jnp.broadcast_to(chunk_decay[:, None, None], (RET_HEADS, 1, LANES))
    lane_head = jnp.arange(LANES) // RET_QK
    mask = (lane_head[None, :] == (jnp.arange(RET_HEADS) % (LANES // RET_QK))[:, None]).astype(F32)
    mask = mask[:, None, :]
    return decay_in, bc(xi), bc(zeta) * mask, mask, cd


def _kv_proj_kernel(m_ref, w_ref, k_ref, v_ref):
    mb = m_ref[...].astype(BF16)
    d = k_ref.shape[1]
    k_ref[...] = jnp.dot(mb, w_ref[:, :d], preferred_element_type=F32).astype(k_ref.dtype)
    v_ref[...] = jnp.dot(mb, w_ref[:, d:], preferred_element_type=F32).astype(v_ref.dtype)


def kv_proj(mem, w_all):
    n, d = mem.shape
    depth = w_all.shape[0]
    out_spec = pl.BlockSpec((None, TM, d), lambda l, i: (l, i, 0))
    return pl.pallas_call(
        _kv_proj_kernel,
        grid=(depth, n // TM),
        in_specs=[pl.BlockSpec((TM, d), lambda l, i: (i, 0)),
                  pl.BlockSpec((None,) + w_all.shape[1:], lambda l, i: (l, 0, 0))],
        out_specs=[out_spec, out_spec],
        out_shape=[jax.ShapeDtypeStruct((depth, n, d), BF16)] * 2,
        compiler_params=_cparams(("parallel", "parallel")),
        name="kv_proj",
    )(mem, w_all)


def _mix_out_xattn_kernel(x_ref, a_ref, b_ref, wm_ref, k_ref, v_ref, wq_ref, wo_ref,
                          g1_ref, b1_ref, g2_ref, b2_ref, o_ref, *scratch):
    half = a_ref.shape[1]
    hrows = x_ref.shape[0] // 2

    def stages(r):
        rows = slice(r * hrows, (r + 1) * hrows)
        y1_ref, x1_ref, y2_ref, q_ref, oh_ref = scratch[5 * r:5 * r + 5]

        def mix_out():
            m = jnp.dot(a_ref[rows, :], wm_ref[:half, :], preferred_element_type=F32)
            m = m + jnp.dot(b_ref[rows, :], wm_ref[half:, :], preferred_element_type=F32)
            y1_ref[...] = DN_ALPHA * x_ref[rows, :] + m

        def norm1():
            x1_ref[...] = _layer_norm(y1_ref[...], g1_ref[...], b1_ref[...])

        def q_proj():
            q = jnp.dot(x1_ref[...].astype(BF16), wq_ref[...], preferred_element_type=F32)
            q_ref[...] = (q * (XA_HEAD ** -0.5)).astype(BF16)

        def attend():
            for h in range(XA_HEADS):
                cols = slice(h * XA_HEAD, (h + 1) * XA_HEAD)
                s = lax.dot_general(q_ref[:, cols], k_ref[:, cols], (((1,), (1,)), ((), ())),
                                    preferred_element_type=F32)
                s = s - jnp.max(s, axis=-1, keepdims=True)
                p = jnp.exp(s)
                p = p / jnp.sum(p, axis=-1, keepdims=True)
                oh_ref[:, cols] = jnp.dot(p.astype(BF16), v_ref[:, cols],
                                          preferred_element_type=F32).astype(BF16)

        def o_proj():
            c = jnp.dot(oh_ref[...], wo_ref[...], preferred_element_type=F32)
            y2_ref[...] = DN_ALPHA * x1_ref[...] + c

        def norm2():
            o_ref[rows, :] = _layer_norm(y2_ref[...], g2_ref[...], b2_ref[...])

        return [mix_out, norm1, q_proj, attend, o_proj, norm2]

    _emit_skewed(stages(0), stages(1))


def mix_out_xattn(x, a, b, w_mix, k, v, wq_all, wo_all, layer, g1, b1, g2, b2, seq):
    n, d = x.shape
    half = a.shape[1]
    tm = min(TM_LN, seq)
    tiles_per_seq = seq // tm
    row_spec = lambda w: pl.BlockSpec((tm, w), lambda i: (i, 0))
    kv_spec = pl.BlockSpec((None, MEM_LEN, d), lambda i: (layer, i // tiles_per_seq, 0))
    vec = _const_spec((1, d))
    return pl.pallas_call(
        _mix_out_xattn_kernel,
        grid=(n // tm,),
        in_specs=[row_spec(d), row_spec(half), row_spec(half), _const_spec(w_mix.shape),
                  kv_spec, kv_spec, _stacked_spec(wq_all, (layer,)), _stacked_spec(wo_all, (layer,)),
                  vec, vec, vec, vec],
        out_specs=row_spec(d),
        out_shape=jax.ShapeDtypeStruct((n, d), F32),
        scratch_shapes=[pltpu.VMEM((tm // 2, d), dt)
                        for _ in range(2) for dt in (F32, F32, F32, BF16, BF16)],
        compiler_params=_cparams(("parallel",)),
        name="mix_out_xattn",
    )(x, a, b, w_mix, k, v, wq_all, wo_all, g1, b1, g2, b2)


def _lru_gate_matrix(gate_w):
    eye = jnp.eye(LRU_BLOCKS, dtype=F32)
    dense = jnp.einsum('gncd,nm->gncmd', gate_w.astype(F32), eye)
    width = LRU_BLOCKS * LRU_BLOCK
    dense = dense.reshape(2, width, width)
    return jnp.concatenate([dense[0], dense[1]], axis=-1).astype(BF16)


def _diff_lambda_init(layer):
    return 0.8 - 0.6 * math.exp(-0.3 * layer)


EVEN_SECTIONS = (
    (0, HALF_MIX, True, DA_HEAD ** -0.5 * math.log2(math.e)),
    (HALF_MIX, HALF_MIX, True, 1.0),
    (2 * HALF_MIX, HALF_MIX, False, 1.0),
    (3 * HALF_MIX, HALF_MIX, False, 1.0),
    (4 * HALF_MIX, HALF_MIX, False, 1.0),
)
EVEN_DTYPES = (BF16, BF16, BF16, F32, F32)
RET_QK_W = RET_HEADS * RET_QK
ODD_SECTIONS = (
    (0, HALF_MIX, False, 1.0),
    (HALF_MIX, RET_QK_W, True, 1.0),
    (HALF_MIX + RET_QK_W, RET_QK_W, True, RET_QK ** -0.5),
    (HALF_MIX + 2 * RET_QK_W, HALF_MIX, False, 1.0),
    (2 * HALF_MIX + 2 * RET_QK_W, HALF_MIX, False, 1.0),
)
ODD_DTYPES = (F32, BF16, BF16, BF16, F32)


def kernel(x, mem, ln_g, ln_b, ffn_w_gate, ffn_w_up, ffn_w_down, xa_w_q, xa_w_kv, xa_w_o,
           ev_w_in, ev_w_out, da_lambda, da_norm_g, lru_conv_w, lru_conv_b, lru_gate_w,
           lru_gate_b, lru_lambda, od_w_in, od_w_out, s5_lam_re, s5_lam_im, s5_log_step,
           s5_b_re, s5_b_im, s5_c_re, s5_c_im, s5_d, s5_glu_w, s5_glu_b, ret_norm_g):
    bsz, seq, d = x.shape
    n = bsz * seq
    depth = ln_g.shape[0]
    mem2 = mem.reshape(bsz * mem.shape[1], d)
    da_tables = _rotary_tables(seq, DA_ROPE, ROPE_THETA, DA_HEAD)
    ret_rot_tables = _rotary_tables(seq, RET_QK, RET_THETA, RET_QK)
    ret_tables = _retention_tables()
    row = lambda v: v.reshape(1, -1).astype(F32)

    wg, wu, wd = ffn_w_gate.astype(BF16), ffn_w_up.astype(BF16), ffn_w_down.astype(BF16)
    wq_all, wo_all = xa_w_q.astype(BF16), xa_w_o.astype(BF16)
    k_all, v_all = kv_proj(mem2, xa_w_kv.astype(BF16))

    h = x.reshape(n, d)
    for l in range(depth):
        h = ffn_ln(h, wg, wu, wd, (l, 0), row(ln_g[l, 0]), row(ln_b[l, 0]))

        if l % 2 == 0:
            e = l // 2
            q, k, v, gate, xr = in_proj(h, ev_w_in[e].astype(BF16), da_tables, EVEN_SECTIONS,
                                        EVEN_DTYPES, DA_ROPE // 2, seq)
            shp = (bsz, seq, HALF_MIX)
            a_out = diff_attn(q.reshape(shp), k.reshape(shp), v.reshape(shp), da_lambda[e],
                              row(da_norm_g[e]), _diff_lambda_init(l))
            b_out = rglru(xr.reshape(shp), gate.reshape(shp), lru_conv_w[e], row(lru_conv_b[e]),
                          _lru_gate_matrix(lru_gate_w[e]), row(lru_gate_b[e]), row(lru_lambda[e]))
            w_out = ev_w_out[e]
        else:
            o = l // 2
            u, q, k, v, g = in_proj(h, od_w_in[o].astype(BF16), ret_rot_tables, ODD_SECTIONS,
                                    ODD_DTYPES, RET_QK // 2, seq)
            shp = (bsz, seq, HALF_MIX)
            wb, a_re, a_im, wc = _s5_params(s5_lam_re[o], s5_lam_im[o], s5_log_step[o],
                                            s5_b_re[o], s5_b_im[o], s5_c_re[o], s5_c_im[o])
            a_out = s5(u.reshape(shp), wb, a_re, a_im, wc, row(s5_d[o]),
                       s5_glu_w[o].astype(BF16), row(s5_glu_b[o]))
            qk_shp = (bsz, seq, RET_QK_W)
            b_out = retention(q.reshape(qk_shp), k.reshape(qk_shp), v.reshape(shp),
                              g.reshape(shp), row(ret_norm_g[o]), ret_tables)
            w_out = od_w_out[o]
        h = mix_out_xattn(h, a_out.reshape(n, HALF_MIX), b_out.reshape(n, HALF_MIX),
                          w_out.astype(BF16), k_all, v_all, wq_all, wo_all, l,
                          row(ln_g[l, 1]), row(ln_b[l, 1]), row(ln_g[l, 2]), row(ln_b[l, 2]), seq)
        h = ffn_ln(h, wg, wu, wd, (l, 1), row(ln_g[l, 3]), row(ln_b[l, 3]))
    return h.reshape(bsz, seq, d)
```

```python
import functools
import math

import jax
import jax.numpy as jnp
from jax import lax
from jax.experimental import pallas as pl
from jax.experimental.pallas import tpu as pltpu

F32 = jnp.float32
BF16 = jnp.bfloat16

D_MODEL = 1024
DEPTH = 4
HALF_MIX = 512
DA_HEADS = 4
DA_HEAD = 64
DA_ROPE = 16
ROPE_THETA = 500000.0
LRU_BLOCKS = 8
LRU_BLOCK = 64
CONV_W = 4
LRU_C = 8.0
S5_GROUP = 16
S5_GROUPS = 32
S5_STATE = 64
RET_HEADS = 4
RET_QK = 64
RET_V = 128
RET_THETA = 10000.0
RET_CHUNK = 128
MEM_LEN = 256
XA_HEADS = 4
XA_HEAD = 256
D_FF = 2816
DN_ALPHA = (2 * DEPTH) ** 0.25
LN_EPS = 1e-5

LANES = 128
SUBLANES = 8
VMEM_LIMIT = 56 * 1024 * 1024

TM = 512
TM_LN = 1024
FF_CHUNK = 256
TQ = 256
SCAN_BATCH = SUBLANES
LRU_T = 256
S5_T = 128
PITCH_PAD = 4
NEG_BIG = -1e30


def _cparams(sem):
    return pltpu.CompilerParams(dimension_semantics=sem, vmem_limit_bytes=VMEM_LIMIT)


def _const_spec(shape):
    nd = len(shape)
    return pl.BlockSpec(shape, lambda *_: (0,) * nd, pipeline_mode=pl.Buffered(1))


def _stacked_spec(arr, idx):
    tail = arr.shape[len(idx):]
    return pl.BlockSpec((None,) * len(idx) + tail, lambda *_: tuple(idx) + (0,) * len(tail),
                        pipeline_mode=pl.Buffered(1))


def _emit_skewed(first, second):
    first[0]()
    for i in range(1, len(first)):
        second[i - 1]()
        first[i]()
    second[-1]()


def _layer_norm(y, g, b):
    mu = jnp.mean(y, axis=-1, keepdims=True)
    yc = y - mu
    var = jnp.mean(yc * yc, axis=-1, keepdims=True)
    return yc * lax.rsqrt(var + LN_EPS) * g + b


def _ffn_kernel(x_ref, wg_ref, wu_ref, wd_ref, g_ref, b_ref, o_ref, xb_ref, a_ref):
    xb_ref[...] = x_ref[...].astype(BF16)
    dff = wg_ref.shape[1]
    for c0 in range(0, dff, FF_CHUNK):
        cols = slice(c0, c0 + FF_CHUNK)
        xb = xb_ref[...]
        g = jnp.dot(xb, wg_ref[:, cols], preferred_element_type=F32)
        u = jnp.dot(xb, wu_ref[:, cols], preferred_element_type=F32)
        a_ref[:, cols] = (g * jax.nn.sigmoid(g) * u).astype(BF16)
    half = x_ref.shape[0] // 2
    for r in range(2):
        rows = slice(r * half, (r + 1) * half)
        acc = jnp.dot(a_ref[rows, :], wd_ref[...], preferred_element_type=F32)
        y = DN_ALPHA * x_ref[rows, :] + 0.5 * acc
        o_ref[rows, :] = _layer_norm(y, g_ref[...], b_ref[...])


def ffn_ln(x, wg, wu, wd, idx, g, b):
    n, d = x.shape
    dff = wg.shape[-1]
    tm = min(TM_LN, n)
    return pl.pallas_call(
        _ffn_kernel,
        grid=(n // tm,),
        in_specs=[
            pl.BlockSpec((tm, d), lambda i: (i, 0)),
            _stacked_spec(wg, idx),
            _stacked_spec(wu, idx),
            _stacked_spec(wd, idx),
            _const_spec((1, d)),
            _const_spec((1, d)),
        ],
        out_specs=pl.BlockSpec((tm, d), lambda i: (i, 0)),
        out_shape=jax.ShapeDtypeStruct((n, d), F32),
        scratch_shapes=[pltpu.VMEM((tm, d), BF16), pltpu.VMEM((tm, dff), BF16)],
        compiler_params=_cparams(("parallel",)),
        name="ffn_ln",
    )(x, wg, wu, wd, g, b)


def _in_proj_kernel(x_ref, w_ref, cos_ref, sa_ref, sb_ref, *o_refs, sections, rot_shift):
    xb = x_ref[...].astype(BF16)
    for (start, width, rotate, scale), o_ref in zip(sections, o_refs):
        z = jnp.dot(xb, w_ref[:, start:start + width], preferred_element_type=F32)
        if rotate:
            cos, sa, sb = cos_ref[...], sa_ref[...], sb_ref[...]
            for j in range(width // LANES):
                zj = z[:, j * LANES:(j + 1) * LANES]
                zr = (zj * cos + pltpu.roll(zj, LANES - rot_shift, 1) * sa
                      + pltpu.roll(zj, rot_shift, 1) * sb)
                o_ref[:, j * LANES:(j + 1) * LANES] = (zr * scale).astype(o_ref.dtype)
        else:
            o_ref[...] = (z * scale).astype(o_ref.dtype)


def in_proj(x, w, tables, sections, out_dtypes, rot_shift, seq):
    n, d = x.shape
    tiles_per_seq = seq // TM
    tab_spec = pl.BlockSpec((TM, LANES), lambda i: (i % tiles_per_seq, 0))
    return pl.pallas_call(
        functools.partial(_in_proj_kernel, sections=sections, rot_shift=rot_shift),
        grid=(n // TM,),
        in_specs=[pl.BlockSpec((TM, d), lambda i: (i, 0)), _const_spec(w.shape),
                  tab_spec, tab_spec, tab_spec],
        out_specs=[pl.BlockSpec((TM, s[1]), lambda i: (i, 0)) for s in sections],
        out_shape=[jax.ShapeDtypeStruct((n, s[1]), dt) for s, dt in zip(sections, out_dtypes)],
        compiler_params=_cparams(("parallel",)),
        name="in_proj",
    )(x, w, *tables)


def _rotary_tables(seq, rot_dim, theta, group):
    half = rot_dim // 2
    inv = theta ** (-jnp.arange(half, dtype=F32) * 2.0 / rot_dim)
    ang = jnp.arange(seq, dtype=jnp.int32).astype(F32)[:, None] * inv[None, :]
    cos, sin = jnp.cos(ang), jnp.sin(ang)
    zeros = jnp.zeros((seq, group - rot_dim), F32)
    zh = jnp.zeros((seq, half), F32)
    cos_g = jnp.concatenate([cos, cos, jnp.ones((seq, group - rot_dim), F32)], -1)
    sa_g = jnp.concatenate([-sin, zh, zeros], -1)
    sb_g = jnp.concatenate([zh, sin, zeros], -1)
    reps = LANES // group
    return tuple(jnp.tile(t, (1, reps)) for t in (cos_g, sa_g, sb_g))


def _diff_attn_kernel(q_ref, k_ref, v_ref, lam_ref, g_ref, o_ref, vx_ref, s0_ref, s1_ref,
                      w0_ref, w1_ref, *, lambda_init):
    seq = q_ref.shape[0]
    tq = TQ
    vx_ref[:, :LANES] = v_ref[...]
    vx_ref[:, LANES:] = jnp.ones((seq, LANES), BF16)
    lam = lam_ref[...]
    lmbda = (jnp.exp(jnp.sum(lam[0:1] * lam[1:2], axis=-1, keepdims=True))
             - jnp.exp(jnp.sum(lam[2:3] * lam[3:4], axis=-1, keepdims=True)) + lambda_init)
    lane = lax.broadcasted_iota(jnp.int32, (tq, LANES), 1)
    row = lax.broadcasted_iota(jnp.int32, (2 * tq, tq), 0)
    col = lax.broadcasted_iota(jnp.int32, (2 * tq, tq), 1)
    causal = col <= jnp.where(row >= tq, row - tq, row)

    s_bufs, w_bufs = (s0_ref, s1_ref), (w0_ref, w1_ref)

    def scores(qi, s_buf):
        qf = q_ref[qi * tq:(qi + 1) * tq, :].astype(F32)
        qq = jnp.concatenate([jnp.where(lane < DA_HEAD, qf, 0.0),
                              jnp.where(lane >= DA_HEAD, qf, 0.0)], axis=0).astype(BF16)
        m_run = None
        for j in range(qi + 1):
            s = lax.dot_general(qq, k_ref[j * tq:(j + 1) * tq, :], (((1,), (1,)), ((), ())),
                                preferred_element_type=F32)
            if j == qi:
                s = jnp.where(causal, s, NEG_BIG)
            s_buf[:, j * tq:(j + 1) * tq] = s
            for c in range(tq // LANES):
                sc = s[:, c * LANES:(c + 1) * LANES]
                m_run = sc if m_run is None else jnp.maximum(m_run, sc)
        return jnp.max(m_run, axis=-1, keepdims=True)

    def finish(qi, m, s_buf, p_buf):
        for j in range(qi + 1):
            cols = slice(j * tq, (j + 1) * tq)
            p_buf[:, cols] = jnp.exp2(s_buf[:, cols] - m).astype(BF16)
        kv = (qi + 1) * tq
        pv = jnp.dot(p_buf[:, :kv], vx_ref[:kv, :], preferred_element_type=F32)
        o = pv[:, :LANES] / pv[:, LANES:]
        d = o[:tq] - lmbda * o[tq:]
        rms = lax.rsqrt(jnp.mean(d * d, axis=-1, keepdims=True) + LN_EPS)
        o_ref[qi * tq:(qi + 1) * tq, :] = (
            d * rms * g_ref[...] * (1.0 - lambda_init)).astype(o_ref.dtype)

    order = list(reversed(range(seq // tq)))
    pending = None
    for n, qi in enumerate(order):
        m = scores(qi, s_bufs[n % 2])
        if pending is not None:
            finish(*pending)
        pending = (qi, m, s_bufs[n % 2], w_bufs[n % 2])
    finish(*pending)


def diff_attn(q, k, v, lam, norm_g, lambda_init):
    bsz, seq, width = q.shape
    heads = width // LANES
    spec = pl.BlockSpec((None, seq, LANES), lambda b, h: (b, 0, h))
    return pl.pallas_call(
        functools.partial(_diff_attn_kernel, lambda_init=lambda_init),
        grid=(bsz, heads),
        in_specs=[spec, spec, spec, _const_spec(lam.shape), _const_spec(norm_g.shape)],
        out_specs=spec,
        out_shape=jax.ShapeDtypeStruct((bsz, seq, width), BF16),
        scratch_shapes=[pltpu.VMEM((seq, 2 * LANES), BF16),
                        pltpu.VMEM((2 * TQ, seq), F32), pltpu.VMEM((2 * TQ, seq), F32),
                        pltpu.VMEM((2 * TQ, seq), BF16), pltpu.VMEM((2 * TQ, seq), BF16)],
        compiler_params=_cparams(("parallel", "parallel")),
        name="diff_attn",
    )(q, k, v, lam, norm_g)


def _pitch(t_steps):
    return t_steps + PITCH_PAD


def _rows(t_steps):
    return SCAN_BATCH * _pitch(t_steps)


def _seq_row0(b, t_steps):
    return b * _pitch(t_steps) + PITCH_PAD


def _time_rows(t, t_steps):
    return pl.ds(PITCH_PAD + t, SCAN_BATCH, stride=_pitch(t_steps))


CONV_HEAD = SUBLANES


def _rglru_kernel(xr_ref, gate_ref, cw_ref, cb_ref, wg_ref, gb_ref, lam_ref, o_ref,
                  xpad_ref, a_ref, b_ref, h_ref, *, t_steps, row_chunks):
    ti = pl.program_id(1)
    pitch, rows = _pitch(t_steps), _rows(t_steps)
    width = xr_ref.shape[2]
    nslab = width // LANES
    hist = CONV_W - 1

    @pl.when(ti == 0)
    def _():
        xpad_ref[...] = jnp.zeros_like(xpad_ref)
        h_ref[...] = jnp.zeros_like(h_ref)

    @pl.when(ti > 0)
    def _():
        for b in range(SCAN_BATCH):
            r0 = CONV_HEAD + _seq_row0(b, t_steps)
            xpad_ref[pl.ds(r0 - hist, hist), :] = xpad_ref[pl.ds(r0 + t_steps - hist, hist), :]

    for b in range(SCAN_BATCH):
        xpad_ref[pl.ds(CONV_HEAD + _seq_row0(b, t_steps), t_steps), :] = xr_ref[b]

    softplus = jax.nn.softplus(-lam_ref[...])
    chunk = rows // row_chunks
    for c in range(row_chunks):
        r0 = c * chunk
        xc = cb_ref[...] + cw_ref[0:1, :] * xpad_ref[pl.ds(CONV_HEAD + r0 - hist, chunk), :]
        for j in range(1, CONV_W):
            xc = xc + cw_ref[j:j + 1, :] * xpad_ref[pl.ds(CONV_HEAD + r0 - hist + j, chunk), :]
        gates = jnp.dot(xc.astype(BF16), wg_ref[...], preferred_element_type=F32) + gb_ref[...]
        r = jax.nn.sigmoid(gates[:, :width])
        i = jax.nn.sigmoid(gates[:, width:])
        log_a = -LRU_C * r * softplus
        a = jnp.exp(log_a)
        bb = jnp.sqrt(-jnp.tanh(log_a) * (a * a + 1.0)) * (i * xc)
        for s in range(nslab):
            a_ref[s, pl.ds(r0, chunk), :] = a[:, s * LANES:(s + 1) * LANES]
            b_ref[s, pl.ds(r0, chunk), :] = bb[:, s * LANES:(s + 1) * LANES]

    def step(t, hs):
        new = []
        for s in range(nslab):
            rows_t = _time_rows(t, t_steps)
            h = a_ref[s, rows_t, :] * hs[s] + b_ref[s, rows_t, :]
            b_ref[s, rows_t, :] = h
            new.append(h)
        return tuple(new)

    hs = lax.fori_loop(0, t_steps, step, tuple(h_ref[s] for s in range(nslab)), unroll=8)
    for s in range(nslab):
        h_ref[s] = hs[s]

    for b in range(SCAN_BATCH):
        r0 = _seq_row0(b, t_steps)
        for s in range(nslab):
            h = b_ref[s, pl.ds(r0, t_steps), :]
            gt = gate_ref[b, :, s * LANES:(s + 1) * LANES]
            o_ref[b, :, s * LANES:(s + 1) * LANES] = (jax.nn.gelu(gt) * h).astype(o_ref.dtype)


def rglru(xr, gate, conv_w, conv_b, w_gates, gate_b, lru_lam, t_steps=None):
    bsz, seq, width = xr.shape
    t_steps = t_steps or min(LRU_T, seq)
    rows = _rows(t_steps)
    row_chunks = next(c for c in (5, 4, 3, 2, 1) if (rows // SUBLANES) % c == 0)
    blk = pl.BlockSpec((SCAN_BATCH, t_steps, width), lambda g, t: (g, t, 0))
    return pl.pallas_call(
        functools.partial(_rglru_kernel, t_steps=t_steps, row_chunks=row_chunks),
        grid=(bsz // SCAN_BATCH, seq // t_steps),
        in_specs=[blk, blk, _const_spec(conv_w.shape), _const_spec(conv_b.shape),
                  _const_spec(w_gates.shape), _const_spec(gate_b.shape),
                  _const_spec(lru_lam.shape)],
        out_specs=blk,
        out_shape=jax.ShapeDtypeStruct((bsz, seq, width), BF16),
        scratch_shapes=[pltpu.VMEM((CONV_HEAD + rows, width), F32),
                        pltpu.VMEM((width // LANES, rows, LANES), F32),
                        pltpu.VMEM((width // LANES, rows, LANES), F32),
                        pltpu.VMEM((width // LANES, SCAN_BATCH, LANES), F32)],
        compiler_params=_cparams(("parallel", "arbitrary")),
        name="rglru",
    )(xr, gate, conv_w, conv_b, w_gates, gate_b, lru_lam)


S5_PAIRS = S5_GROUPS // 2
S5_PAIRS_PER_PASS = 4
S5_PAIRS_PER_BLOCK = LANES // (2 * S5_GROUP)


def _s5_kernel(u_ref, wb_ref, are_ref, aim_ref, wc_ref, d_ref, wglu_ref, bglu_ref, o_ref,
               upad_ref, xs_ref, h_ref, y_ref, *, t_steps):
    ti = pl.program_id(1)
    width = u_ref.shape[2]
    ppp = S5_PAIRS_PER_PASS
    n_pass = S5_PAIRS // ppp

    @pl.when(ti == 0)
    def _():
        upad_ref[...] = jnp.zeros_like(upad_ref)
        h_ref[...] = jnp.zeros_like(h_ref)

    for b in range(SCAN_BATCH):
        upad_ref[pl.ds(_seq_row0(b, t_steps), t_steps), :] = u_ref[b]

    def b_proj(p):
        blk = p // S5_PAIRS_PER_BLOCK
        ub = upad_ref[:, blk * LANES:(blk + 1) * LANES].astype(BF16)
        x = jnp.dot(ub, wb_ref[p], preferred_element_type=F32)
        xs_ref[2 * p] = x[:, :LANES]
        xs_ref[2 * p + 1] = x[:, LANES:]

    def c_proj(p):
        q, first, last = p // ppp, p % ppp == 0, p % ppp == ppp - 1
        lanes = slice(q * LANES, (q + 1) * LANES)
        hcat = jnp.concatenate([xs_ref[2 * p], xs_ref[2 * p + 1]], axis=-1).astype(BF16)
        y = jnp.dot(hcat, wc_ref[p], preferred_element_type=F32)
        y = y + (d_ref[:, lanes] * upad_ref[:, lanes] if first else y_ref[:, lanes])
        y_ref[:, lanes] = jax.nn.gelu(y) if last else y

    def scan(pairs, a_re, a_im, hs, t0, t1):
        for t in range(t0, t1):
            rows_t = _time_rows(t, t_steps)
            new = []
            for n, p in enumerate(pairs):
                h_re, h_im = hs[2 * n], hs[2 * n + 1]
                n_re = a_re[n] * h_re - a_im[n] * h_im + xs_ref[2 * p, rows_t, :]
                n_im = a_re[n] * h_im + a_im[n] * h_re + xs_ref[2 * p + 1, rows_t, :]
                xs_ref[2 * p, rows_t, :] = n_re
                xs_ref[2 * p + 1, rows_t, :] = n_im
                new += [n_re, n_im]
            hs = tuple(new)
        return hs

    for p in range(ppp):
        b_proj(p)
    for k in range(n_pass):
        pairs = range(k * ppp, (k + 1) * ppp)
        a_re = [jnp.broadcast_to(are_ref[p:p + 1, :], (SCAN_BATCH, LANES)) for p in pairs]
        a_im = [jnp.broadcast_to(aim_ref[p:p + 1, :], (SCAN_BATCH, LANES)) for p in pairs]
        hs = tuple(h_ref[s] for p in pairs for s in (2 * p, 2 * p + 1))
        side = []
        if k + 1 < n_pass:
            side += [functools.partial(b_proj, p + ppp) for p in pairs]
        if k > 0:
            side += [functools.partial(c_proj, p - ppp) for p in pairs]
        seg = t_steps // len(side)
        for i, matmul in enumerate(side):
            matmul()
            hs = scan(pairs, a_re, a_im, hs, i * seg, t_steps if i + 1 == len(side) else (i + 1) * seg)
        for n, p in enumerate(pairs):
            h_ref[2 * p] = hs[2 * n]
            h_ref[2 * p + 1] = hs[2 * n + 1]
    for p in range((n_pass - 1) * ppp, n_pass * ppp):
        c_proj(p)

    z = y_ref[...]
    gate = jnp.dot(z.astype(BF16), wglu_ref[...], preferred_element_type=F32) + bglu_ref[...]
    y_ref[...] = z * jax.nn.sigmoid(gate)
    for b in range(SCAN_BATCH):
        o_ref[b] = y_ref[pl.ds(_seq_row0(b, t_steps), t_steps), :].astype(o_ref.dtype)


def s5(u, wb, a_re, a_im, wc, d_skip, w_glu, b_glu, t_steps=None):
    bsz, seq, width = u.shape
    t_steps = t_steps or min(S5_T, seq)
    rows = _rows(t_steps)
    blk = pl.BlockSpec((SCAN_BATCH, t_steps, width), lambda g, t: (g, t, 0))
    nslab = 2 * S5_PAIRS
    return pl.pallas_call(
        functools.partial(_s5_kernel, t_steps=t_steps),
        grid=(bsz // SCAN_BATCH, seq // t_steps),
        in_specs=[blk, _const_spec(wb.shape), _const_spec(a_re.shape), _const_spec(a_im.shape),
                  _const_spec(wc.shape), _const_spec(d_skip.shape), _const_spec(w_glu.shape),
                  _const_spec(b_glu.shape)],
        out_specs=blk,
        out_shape=jax.ShapeDtypeStruct((bsz, seq, width), BF16),
        scratch_shapes=[pltpu.VMEM((rows, width), F32),
                        pltpu.VMEM((nslab, rows, LANES), F32),
                        pltpu.VMEM((nslab, SCAN_BATCH, LANES), F32),
                        pltpu.VMEM((rows, width), F32)],
        compiler_params=_cparams(("parallel", "arbitrary")),
        name="s5",
    )(u, wb, a_re, a_im, wc, d_skip, w_glu, b_glu)


def _s5_params(lam_re, lam_im, log_step, b_re, b_im, c_re, c_im):
    step = jnp.exp(log_step.astype(F32))[:, None]
    lr = jnp.minimum(lam_re.astype(F32), -1e-4)
    li = lam_im.astype(F32)
    mag = jnp.exp(lr * step)
    ang = li * step
    ab_re, ab_im = mag * jnp.cos(ang), mag * jnp.sin(ang)
    den = lr * lr + li * li
    nr, ni = ab_re - 1.0, ab_im
    f_re = (nr * lr + ni * li) / den
    f_im = (ni * lr - nr * li) / den
    bb_re = f_re[:, :, None] * b_re - f_im[:, :, None] * b_im
    bb_im = f_re[:, :, None] * b_im + f_im[:, :, None] * b_re
    g, n, c = bb_re.shape
    eye2 = jnp.eye(2, dtype=F32)
    eye_b = jnp.eye(S5_PAIRS_PER_BLOCK, dtype=F32)

    def in_mat(bb):
        bt = bb.transpose(0, 2, 1).reshape(S5_PAIRS, 2, c, n)
        m = jnp.einsum('pgcn,gh->pgchn', bt, eye2).reshape(S5_PAIRS, 2 * c, 2 * n)
        slot = jnp.arange(S5_PAIRS) % S5_PAIRS_PER_BLOCK
        sel = eye_b[slot]
        return jnp.einsum('pkn,ps->pskn', m, sel).reshape(S5_PAIRS, LANES, 2 * n)

    wb = jnp.concatenate([in_mat(bb_re), in_mat(bb_im)], axis=-1).astype(BF16)

    pairs_per_out = S5_PAIRS // (HALF_MIX // LANES)
    eye_o = jnp.eye(pairs_per_out, dtype=F32)

    def out_mat(cc):
        ct = cc.transpose(0, 2, 1).reshape(S5_PAIRS, 2, n, c)
        m = jnp.einsum('pgnc,gh->pgnhc', ct, eye2).reshape(S5_PAIRS, 2 * n, 2 * c)
        slot = jnp.arange(S5_PAIRS) % pairs_per_out
        sel = eye_o[slot]
        return jnp.einsum('pnk,ps->pnsk', m, sel).reshape(S5_PAIRS, 2 * n, LANES)

    wc = jnp.concatenate([out_mat(c_re.astype(F32)), -out_mat(c_im.astype(F32))],
                         axis=1).astype(BF16)
    a_re = ab_re.reshape(S5_PAIRS, 2 * n)
    a_im = ab_im.reshape(S5_PAIRS, 2 * n)
    return wb, a_re, a_im, wc


def _retention_kernel(q_ref, k_ref, v_ref, g_ref, ng_ref, din_ref, xi_ref, zeta_ref, mask_ref,
                      cd_ref, o_ref):
    seq = q_ref.shape[0]
    c = RET_CHUNK
    nc = seq // c
    heads = LANES // RET_QK
    r = [None] * heads
    for ci in range(nc):
        rows = slice(ci * c, (ci + 1) * c)
        qb = q_ref[rows, :]
        kf = k_ref[rows, :].astype(F32)
        for hh in range(heads):
            lanes = slice(hh * RET_V, (hh + 1) * RET_V)
            vb = v_ref[rows, lanes]
            inner = lax.dot_general(qb, (kf * mask_ref[hh]).astype(BF16), (((1,), (1,)), ((), ())),
                                    preferred_element_type=F32) * din_ref[hh]
            o = jnp.dot(inner.astype(BF16), vb, preferred_element_type=F32)
            if r[hh] is not None:
                o = o + jnp.dot(qb, r[hh].astype(BF16), preferred_element_type=F32) * xi_ref[hh]
            if ci + 1 < nc:
                kz = (kf * zeta_ref[hh]).astype(BF16)
                upd = lax.dot_general(kz, vb, (((0,), (0,)), ((), ())),
                                      preferred_element_type=F32)
                r[hh] = upd if r[hh] is None else cd_ref[hh] * r[hh] + upd
            mu = jnp.mean(o, axis=-1, keepdims=True)
            oc = o - mu
            var = jnp.mean(oc * oc, axis=-1, keepdims=True)
            on = oc * lax.rsqrt(var + LN_EPS) * ng_ref[...]
            gt = g_ref[rows, lanes]
            o_ref[rows, lanes] = (gt * jax.nn.sigmoid(gt) * on).astype(o_ref.dtype)


def retention(q, k, v, g, norm_g, tables):
    bsz, seq, _ = q.shape
    width = v.shape[2]
    heads = LANES // RET_QK
    pairs = width // (heads * RET_V)
    qk_spec = pl.BlockSpec((None, seq, LANES), lambda b, p: (b, 0, p))
    v_spec = pl.BlockSpec((None, seq, heads * RET_V), lambda b, p: (b, 0, p))
    tables = [t.reshape((pairs, heads) + t.shape[1:]) for t in tables]
    tab_specs = [pl.BlockSpec((None,) + t.shape[1:], lambda b, p: (p, 0, 0, 0)) for t in tables]
    return pl.pallas_call(
        _retention_kernel,
        grid=(bsz, pairs),
        in_specs=[qk_spec, qk_spec, v_spec, v_spec, _const_spec(norm_g.shape)] + tab_specs,
        out_specs=v_spec,
        out_shape=jax.ShapeDtypeStruct((bsz, seq, width), BF16),
        compiler_params=_cparams(("parallel", "parallel")),
        name="retention",
    )(q, k, v, g, norm_g, *tables)


def _retention_tables():
    c = RET_CHUNK
    log_g = jnp.log(1.0 - jnp.exp2(-5.0 - jnp.arange(RET_HEADS, dtype=F32)))
    idx = jnp.arange(c, dtype=F32)
    diff = idx[:, None] - idx[None, :]
    causal = diff >= 0
    decay_in = jnp.where(causal[None], jnp.exp(log_g[:, None, None] * jnp.where(causal, diff, 0.0)[None]), 0.0)
    xi = jnp.exp(log_g[:, None] * (idx + 1.0))
    zeta = jnp.exp(log_g[:, None] * (c - 1.0 - idx))
    chunk_decay = jnp.exp(log_g * c)
    bc = lambda t: jnp.broadcast_to(t[:, :, None], (RET_HEADS, c, LANES))
    cd = jnp.broadcast_to(chunk_decay[:, None, None], (RET_HEADS, 1, LANES))
    lane_head = jnp.arange(LANES) // RET_QK
    mask = (lane_head[None, :] == (jnp.arange(RET_HEADS) % (LANES // RET_QK))[:, None]).astype(F32)
    mask = mask[:, None, :]
    return decay_in, bc(xi), bc(zeta) * mask, mask, cd


def _kv_proj_kernel(m_ref, w_ref, k_ref, v_ref):
    mb = m_ref[...].astype(BF16)
    d = k_ref.shape[1]
    k_ref[...] = jnp.dot(mb, w_ref[:, :d], preferred_element_type=F32).astype(k_ref.dtype)
    v_ref[...] = jnp.dot(mb, w_ref[:, d:], preferred_element_type=F32).astype(v_ref.dtype)


def kv_proj(mem, w_all):
    n, d = mem.shape
    depth = w_all.shape[0]
    out_spec = pl.BlockSpec((None, TM, d), lambda l, i: (l, i, 0))
    return pl.pallas_call(
        _kv_proj_kernel,
        grid=(depth, n // TM),
        in_specs=[pl.BlockSpec((TM, d), lambda l, i: (i, 0)),
                  pl.BlockSpec((None,) + w_all.shape[1:], lambda l, i: (l, 0, 0))],
        out_specs=[out_spec, out_spec],
        out_shape=[jax.ShapeDtypeStruct((depth, n, d), BF16)] * 2,
        compiler_params=_cparams(("parallel", "parallel")),
        name="kv_proj",
    )(mem, w_all)


def _mix_out_xattn_kernel(x_ref, a_ref, b_ref, wm_ref, k_ref, v_ref, wq_ref, wo_ref,
                          g1_ref, b1_ref, g2_ref, b2_ref, o_ref, *scratch):
    half = a_ref.shape[1]
    hrows = x_ref.shape[0] // 2

    def stages(r):
        rows = slice(r * hrows, (r + 1) * hrows)
        y1_ref, x1_ref, y2_ref, q_ref, oh_ref = scratch[5 * r:5 * r + 5]

        def mix_out():
            m = jnp.dot(a_ref[rows, :], wm_ref[:half, :], preferred_element_type=F32)
            m = m + jnp.dot(b_ref[rows, :], wm_ref[half:, :], preferred_element_type=F32)
            y1_ref[...] = DN_ALPHA * x_ref[rows, :] + m

        def norm1():
            x1_ref[...] = _layer_norm(y1_ref[...], g1_ref[...], b1_ref[...])

        def q_proj():
            q = jnp.dot(x1_ref[...].astype(BF16), wq_ref[...], preferred_element_type=F32)
            q_ref[...] = (q * (XA_HEAD ** -0.5)).astype(BF16)

        def attend():
            for h in range(XA_HEADS):
                cols = slice(h * XA_HEAD, (h + 1) * XA_HEAD)
                s = lax.dot_general(q_ref[:, cols], k_ref[:, cols], (((1,), (1,)), ((), ())),
                                    preferred_element_type=F32)
                s = s - jnp.max(s, axis=-1, keepdims=True)
                p = jnp.exp(s)
                p = p / jnp.sum(p, axis=-1, keepdims=True)
                oh_ref[:, cols] = jnp.dot(p.astype(BF16), v_ref[:, cols],
                                          preferred_element_type=F32).astype(BF16)

        def o_proj():
            c = jnp.dot(oh_ref[...], wo_ref[...], preferred_element_type=F32)
            y2_ref[...] = DN_ALPHA * x1_ref[...] + c

        def norm2():
            o_ref[rows, :] = _layer_norm(y2_ref[...], g2_ref[...], b2_ref[...])

        return [mix_out, norm1, q_proj, attend, o_proj, norm2]

    _emit_skewed(stages(0), stages(1))


def mix_out_xattn(x, a, b, w_mix, k, v, wq_all, wo_all, layer, g1, b1, g2, b2, seq):
    n, d = x.shape
    half = a.shape[1]
    tm = min(TM_LN, seq)
    tiles_per_seq = seq // tm
    row_spec = lambda w: pl.BlockSpec((tm, w), lambda i: (i, 0))
    kv_spec = pl.BlockSpec((None, MEM_LEN, d), lambda i: (layer, i // tiles_per_seq, 0))
    vec = _const_spec((1, d))
    return pl.pallas_call(
        _mix_out_xattn_kernel,
        grid=(n // tm,),
        in_specs=[row_spec(d), row_spec(half), row_spec(half), _const_spec(w_mix.shape),
                  kv_spec, kv_spec, _stacked_spec(wq_all, (layer,)), _stacked_spec(wo_all, (layer,)),
                  vec, vec, vec, vec],
        out_specs=row_spec(d),
        out_shape=jax.ShapeDtypeStruct((n, d), F32),
        scratch_shapes=[pltpu.VMEM((tm // 2, d), dt)
                        for _ in range(2) for dt in (F32, F32, F32, BF16, BF16)],
        compiler_params=_cparams(("parallel",)),
        name="mix_out_xattn",
    )(x, a, b, w_mix, k, v, wq_all, wo_all, g1, b1, g2, b2)


def _lru_gate_matrix(gate_w):
    eye = jnp.eye(LRU_BLOCKS, dtype=F32)
    dense = jnp.einsum('gncd,nm->gncmd', gate_w.astype(F32), eye)
    width = LRU_BLOCKS * LRU_BLOCK
    dense = dense.reshape(2, width, width)
    return jnp.concatenate([dense[0], dense[1]], axis=-1).astype(BF16)


def _diff_lambda_init(layer):
    return 0.8 - 0.6 * math.exp(-0.3 * layer)


EVEN_SECTIONS = (
    (0, HALF_MIX, True, DA_HEAD ** -0.5 * math.log2(math.e)),
    (HALF_MIX, HALF_MIX, True, 1.0),
    (2 * HALF_MIX, HALF_MIX, False, 1.0),
    (3 * HALF_MIX, HALF_MIX, False, 1.0),
    (4 * HALF_MIX, HALF_MIX, False, 1.0),
)
EVEN_DTYPES = (BF16, BF16, BF16, F32, F32)
RET_QK_W = RET_HEADS * RET_QK
ODD_SECTIONS = (
    (0, HALF_MIX, False, 1.0),
    (HALF_MIX, RET_QK_W, True, 1.0),
    (HALF_MIX + RET_QK_W, RET_QK_W, True, RET_QK ** -0.5),
    (HALF_MIX + 2 * RET_QK_W, HALF_MIX, False, 1.0),
    (2 * HALF_MIX + 2 * RET_QK_W, HALF_MIX, False, 1.0),
)
ODD_DTYPES = (F32, BF16, BF16, BF16, F32)


def kernel(x, mem, ln_g, ln_b, ffn_w_gate, ffn_w_up, ffn_w_down, xa_w_q, xa_w_kv, xa_w_o,
           ev_w_in, ev_w_out, da_lambda, da_norm_g, lru_conv_w, lru_conv_b, lru_gate_w,
           lru_gate_b, lru_lambda, od_w_in, od_w_out, s5_lam_re, s5_lam_im, s5_log_step,
           s5_b_re, s5_b_im, s5_c_re, s5_c_im, s5_d, s5_glu_w, s5_glu_b, ret_norm_g):
    bsz, seq, d = x.shape
    n = bsz * seq
    depth = ln_g.shape[0]
    mem2 = mem.reshape(bsz * mem.shape[1], d)
    da_tables = _rotary_tables(seq, DA_ROPE, ROPE_THETA, DA_HEAD)
    ret_rot_tables = _rotary_tables(seq, RET_QK, RET_THETA, RET_QK)
    ret_tables = _retention_tables()
    row = lambda v: v.reshape(1, -1).astype(F32)

    wg, wu, wd = ffn_w_gate.astype(BF16), ffn_w_up.astype(BF16), ffn_w_down.astype(BF16)
    wq_all, wo_all = xa_w_q.astype(BF16), xa_w_o.astype(BF16)
    k_all, v_all = kv_proj(mem2, xa_w_kv.astype(BF16))

    h = x.reshape(n, d)
    for l in range(depth):
        h = ffn_ln(h, wg, wu, wd, (l, 0), row(ln_g[l, 0]), row(ln_b[l, 0]))

        if l % 2 == 0:
            e = l // 2
            q, k, v, gate, xr = in_proj(h, ev_w_in[e].astype(BF16), da_tables, EVEN_SECTIONS,
                                        EVEN_DTYPES, DA_ROPE // 2, seq)
            shp = (bsz, seq, HALF_MIX)
            a_out = diff_attn(q.reshape(shp), k.reshape(shp), v.reshape(shp), da_lambda[e],
                              row(da_norm_g[e]), _diff_lambda_init(l))
            b_out = rglru(xr.reshape(shp), gate.reshape(shp), lru_conv_w[e], row(lru_conv_b[e]),
                          _lru_gate_matrix(lru_gate_w[e]), row(lru_gate_b[e]), row(lru_lambda[e]))
            w_out = ev_w_out[e]
        else:
            o = l // 2
            u, q, k, v, g = in_proj(h, od_w_in[o].astype(BF16), ret_rot_tables, ODD_SECTIONS,
                                    ODD_DTYPES, RET_QK // 2, seq)
            shp = (bsz, seq, HALF_MIX)
            wb, a_re, a_im, wc = _s5_params(s5_lam_re[o], s5_lam_im[o], s5_log_step[o],
                                            s5_b_re[o], s5_b_im[o], s5_c_re[o], s5_c_im[o])
            a_out = s5(u.reshape(shp), wb, a_re, a_im, wc, row(s5_d[o]),
                       s5_glu_w[o].astype(BF16), row(s5_glu_b[o]))
            qk_shp = (bsz, seq, RET_QK_W)
            b_out = retention(q.reshape(qk_shp), k.reshape(qk_shp), v.reshape(shp),
                              g.reshape(shp), row(ret_norm_g[o]), ret_tables)
            w_out = od_w_out[o]
        h = mix_out_xattn(h, a_out.reshape(n, HALF_MIX), b_out.reshape(n, HALF_MIX),
                          w_out.astype(BF16), k_all, v_all, wq_all, wo_all, l,
                          row(ln_g[l, 1]), row(ln_b[l, 1]), row(ln_g[l, 2]), row(ln_b[l, 2]), seq)
        h = ffn_ln(h, wg, wu, wd, (l, 1), row(ln_g[l, 3]), row(ln_b[l, 3]))
    return h.reshape(bsz, seq, d)
```

```python
import functools
import math

import jax
import jax.numpy as jnp
from jax import lax
from jax.experimental import pallas as pl
from jax.experimental.pallas import tpu as pltpu

F32 = jnp.float32
BF16 = jnp.bfloat16

D_MODEL = 1024
DEPTH = 4
HALF_MIX = 512
DA_HEADS = 4
DA_HEAD = 64
DA_ROPE = 16
ROPE_THETA = 500000.0
LRU_BLOCKS = 8
LRU_BLOCK = 64
CONV_W = 4
LRU_C = 8.0
S5_GROUP = 16
S5_GROUPS = 32
S5_STATE = 64
RET_HEADS = 4
RET_QK = 64
RET_V = 128
RET_THETA = 10000.0
RET_CHUNK = 128
MEM_LEN = 256
XA_HEADS = 4
XA_HEAD = 256
D_FF = 2816
DN_ALPHA = (2 * DEPTH) ** 0.25
LN_EPS = 1e-5

LANES = 128
SUBLANES = 8
VMEM_LIMIT = 56 * 1024 * 1024

TM = 1024
FF_CHUNK = 256
TQ = 256
SCAN_BATCH = SUBLANES
LRU_T = 256
S5_T = 128
PITCH_PAD = 4
NEG_BIG = -1e30


def _cparams(sem):
    return pltpu.CompilerParams(dimension_semantics=sem, vmem_limit_bytes=VMEM_LIMIT)


def _const_spec(shape):
    nd = len(shape)
    return pl.BlockSpec(shape, lambda *_: (0,) * nd, pipeline_mode=pl.Buffered(1))


def _stacked_spec(arr, idx):
    tail = arr.shape[len(idx):]
    return pl.BlockSpec((None,) * len(idx) + tail, lambda *_: tuple(idx) + (0,) * len(tail),
                        pipeline_mode=pl.Buffered(1))


def _emit_skewed(first, second):
    first[0]()
    for i in range(1, len(first)):
        second[i - 1]()
        first[i]()
    second[-1]()


def _sigmoid(x):
    return 0.5 * jnp.tanh(0.5 * x) + 0.5


def _layer_norm(y, g, b):
    mu = jnp.mean(y, axis=-1, keepdims=True)
    yc = y - mu
    var = jnp.mean(yc * yc, axis=-1, keepdims=True)
    return yc * lax.rsqrt(var + LN_EPS) * g + b


def _ffn_kernel(x_ref, wg_ref, wu_ref, wd_ref, g_ref, b_ref, o_ref, xb_ref, a_ref):
    xb_ref[...] = x_ref[...].astype(BF16)
    dff = wg_ref.shape[1]
    for c0 in range(0, dff, FF_CHUNK):
        cols = slice(c0, c0 + FF_CHUNK)
        xb = xb_ref[...]
        g = jnp.dot(xb, wg_ref[:, cols], preferred_element_type=F32)
        u = jnp.dot(xb, wu_ref[:, cols], preferred_element_type=F32)
        a_ref[:, cols] = (g * jax.nn.sigmoid(g) * u).astype(BF16)
    half = x_ref.shape[0] // 2
    for r in range(2):
        rows = slice(r * half, (r + 1) * half)
        acc = jnp.dot(a_ref[rows, :], wd_ref[...], preferred_element_type=F32)
        y = DN_ALPHA * x_ref[rows, :] + 0.5 * acc
        o_ref[rows, :] = _layer_norm(y, g_ref[...], b_ref[...])


def ffn_ln(x, wg, wu, wd, idx, g, b):
    n, d = x.shape
    dff = wg.shape[-1]
    tm = min(TM, n)
    return pl.pallas_call(
        _ffn_kernel,
        grid=(n // tm,),
        in_specs=[
            pl.BlockSpec((tm, d), lambda i: (i, 0)),
            _stacked_spec(wg, idx),
            _stacked_spec(wu, idx),
            _stacked_spec(wd, idx),
            _const_spec((1, d)),
            _const_spec((1, d)),
        ],
        out_specs=pl.BlockSpec((tm, d), lambda i: (i, 0)),
        out_shape=jax.ShapeDtypeStruct((n, d), F32),
        scratch_shapes=[pltpu.VMEM((tm, d), BF16), pltpu.VMEM((tm, dff), BF16)],
        compiler_params=_cparams(("parallel",)),
        name="ffn_ln",
    )(x, wg, wu, wd, g, b)


def _in_proj_kernel(x_ref, w_ref, cos_ref, sa_ref, sb_ref, *o_refs, sections, rot_shift):
    xb = x_ref[...].astype(BF16)
    for (start, width, rotate, scale), o_ref in zip(sections, o_refs):
        z = jnp.dot(xb, w_ref[:, start:start + width], preferred_element_type=F32)
        if rotate:
            cos, sa, sb = cos_ref[...], sa_ref[...], sb_ref[...]
            for j in range(width // LANES):
                zj = z[:, j * LANES:(j + 1) * LANES]
                zr = (zj * cos + pltpu.roll(zj, LANES - rot_shift, 1) * sa
                      + pltpu.roll(zj, rot_shift, 1) * sb)
                o_ref[:, j * LANES:(j + 1) * LANES] = (zr * scale).astype(o_ref.dtype)
        else:
            o_ref[...] = (z * scale).astype(o_ref.dtype)


def in_proj(x, w, tables, sections, out_dtypes, rot_shift, seq):
    n, d = x.shape
    tm = min(TM, seq)
    tiles_per_seq = seq // tm
    tab_spec = pl.BlockSpec((tm, LANES), lambda i: (i % tiles_per_seq, 0))
    return pl.pallas_call(
        functools.partial(_in_proj_kernel, sections=sections, rot_shift=rot_shift),
        grid=(n // tm,),
        in_specs=[pl.BlockSpec((tm, d), lambda i: (i, 0)), _const_spec(w.shape),
                  tab_spec, tab_spec, tab_spec],
        out_specs=[pl.BlockSpec((tm, s[1]), lambda i: (i, 0)) for s in sections],
        out_shape=[jax.ShapeDtypeStruct((n, s[1]), dt) for s, dt in zip(sections, out_dtypes)],
        compiler_params=_cparams(("parallel",)),
        name="in_proj",
    )(x, w, *tables)


def _rotary_tables(seq, rot_dim, theta, group):
    half = rot_dim // 2
    inv = theta ** (-jnp.arange(half, dtype=F32) * 2.0 / rot_dim)
    ang = jnp.arange(seq, dtype=jnp.int32).astype(F32)[:, None] * inv[None, :]
    cos, sin = jnp.cos(ang), jnp.sin(ang)
    zeros = jnp.zeros((seq, group - rot_dim), F32)
    zh = jnp.zeros((seq, half), F32)
    cos_g = jnp.concatenate([cos, cos, jnp.ones((seq, group - rot_dim), F32)], -1)
    sa_g = jnp.concatenate([-sin, zh, zeros], -1)
    sb_g = jnp.concatenate([zh, sin, zeros], -1)
    reps = LANES // group
    return tuple(jnp.tile(t, (1, reps)) for t in (cos_g, sa_g, sb_g))


def _diff_attn_kernel(q_ref, k_ref, v_ref, lam_ref, g_ref, o_ref, vx_ref, s0_ref, s1_ref,
                      w0_ref, w1_ref, *, lambda_init):
    seq = q_ref.shape[0]
    tq = TQ
    vx_ref[:, :LANES] = v_ref[...]
    vx_ref[:, LANES:] = jnp.ones((seq, LANES), BF16)
    lam = lam_ref[...]
    lmbda = (jnp.exp(jnp.sum(lam[0:1] * lam[1:2], axis=-1, keepdims=True))
             - jnp.exp(jnp.sum(lam[2:3] * lam[3:4], axis=-1, keepdims=True)) + lambda_init)
    lane = lax.broadcasted_iota(jnp.int32, (tq, LANES), 1)
    row = lax.broadcasted_iota(jnp.int32, (2 * tq, tq), 0)
    col = lax.broadcasted_iota(jnp.int32, (2 * tq, tq), 1)
    causal = col <= jnp.where(row >= tq, row - tq, row)

    s_bufs, w_bufs = (s0_ref, s1_ref), (w0_ref, w1_ref)

    def scores(qi, s_buf):
        qf = q_ref[qi * tq:(qi + 1) * tq, :].astype(F32)
        qq = jnp.concatenate([jnp.where(lane < DA_HEAD, qf, 0.0),
                              jnp.where(lane >= DA_HEAD, qf, 0.0)], axis=0).astype(BF16)
        m_run = None
        for j in range(qi + 1):
            s = lax.dot_general(qq, k_ref[j * tq:(j + 1) * tq, :], (((1,), (1,)), ((), ())),
                                preferred_element_type=F32)
            if j == qi:
                s = jnp.where(causal, s, NEG_BIG)
            s_buf[:, j * tq:(j + 1) * tq] = s
            for c in range(tq // LANES):
                sc = s[:, c * LANES:(c + 1) * LANES]
                m_run = sc if m_run is None else jnp.maximum(m_run, sc)
        return jnp.max(m_run, axis=-1, keepdims=True)

    def finish(qi, m, s_buf, p_buf):
        for j in range(qi + 1):
            cols = slice(j * tq, (j + 1) * tq)
            p_buf[:, cols] = jnp.exp2(s_buf[:, cols] - m).astype(BF16)
        kv = (qi + 1) * tq
        pv = jnp.dot(p_buf[:, :kv], vx_ref[:kv, :], preferred_element_type=F32)
        o = pv[:, :LANES] / pv[:, LANES:]
        d = o[:tq] - lmbda * o[tq:]
        rms = lax.rsqrt(jnp.mean(d * d, axis=-1, keepdims=True) + LN_EPS)
        o_ref[qi * tq:(qi + 1) * tq, :] = (
            d * rms * g_ref[...] * (1.0 - lambda_init)).astype(o_ref.dtype)

    order = list(reversed(range(seq // tq)))
    pending = None
    for n, qi in enumerate(order):
        m = scores(qi, s_bufs[n % 2])
        if pending is not None:
            finish(*pending)
        pending = (qi, m, s_bufs[n % 2], w_bufs[n % 2])
    finish(*pending)


def diff_attn(q, k, v, lam, norm_g, lambda_init):
    bsz, seq, width = q.shape
    heads = width // LANES
    spec = pl.BlockSpec((None, seq, LANES), lambda b, h: (b, 0, h))
    return pl.pallas_call(
        functools.partial(_diff_attn_kernel, lambda_init=lambda_init),
        grid=(bsz, heads),
        in_specs=[spec, spec, spec, _const_spec(lam.shape), _const_spec(norm_g.shape)],
        out_specs=spec,
        out_shape=jax.ShapeDtypeStruct((bsz, seq, width), BF16),
        scratch_shapes=[pltpu.VMEM((seq, 2 * LANES), BF16),
                        pltpu.VMEM((2 * TQ, seq), F32), pltpu.VMEM((2 * TQ, seq), F32),
                        pltpu.VMEM((2 * TQ, seq), BF16), pltpu.VMEM((2 * TQ, seq), BF16)],
        compiler_params=_cparams(("parallel", "parallel")),
        name="diff_attn",
    )(q, k, v, lam, norm_g)


def _pitch(t_steps):
    return t_steps + PITCH_PAD


def _rows(t_steps):
    return SCAN_BATCH * _pitch(t_steps)


def _seq_row0(b, t_steps):
    return b * _pitch(t_steps) + PITCH_PAD


def _time_rows(t, t_steps):
    return pl.ds(PITCH_PAD + t, SCAN_BATCH, stride=_pitch(t_steps))


CONV_HEAD = SUBLANES


def _rglru_kernel(xr_ref, gate_ref, cw_ref, cb_ref, wg_ref, gb_ref, lam_ref, o_ref,
                  xpad_ref, a_ref, b_ref, h_ref, *, t_steps, row_chunks):
    ti = pl.program_id(1)
    pitch, rows = _pitch(t_steps), _rows(t_steps)
    width = xr_ref.shape[2]
    nslab = width // LANES
    hist = CONV_W - 1

    @pl.when(ti == 0)
    def _():
        xpad_ref[...] = jnp.zeros_like(xpad_ref)
        h_ref[...] = jnp.zeros_like(h_ref)

    @pl.when(ti > 0)
    def _():
        for b in range(SCAN_BATCH):
            r0 = CONV_HEAD + _seq_row0(b, t_steps)
            xpad_ref[pl.ds(r0 - hist, hist), :] = xpad_ref[pl.ds(r0 + t_steps - hist, hist), :]

    for b in range(SCAN_BATCH):
        xpad_ref[pl.ds(CONV_HEAD + _seq_row0(b, t_steps), t_steps), :] = xr_ref[b]

    softplus = jax.nn.softplus(-lam_ref[...])
    chunk = rows // row_chunks
    for c in range(row_chunks):
        r0 = c * chunk
        xh = xpad_ref[pl.ds(r0, CONV_HEAD + chunk), :]
        xc = cb_ref[...] + cw_ref[hist:hist + 1, :] * xh[CONV_HEAD:]
        for j in range(hist):
            xc = xc + cw_ref[j:j + 1, :] * pltpu.roll(xh, hist - j, 0)[CONV_HEAD:]
        gates = jnp.dot(xc.astype(BF16), wg_ref[...], preferred_element_type=F32) + gb_ref[...]
        r = _sigmoid(gates[:, :width])
        i = _sigmoid(gates[:, width:])
        log_a = -LRU_C * r * softplus
        a = jnp.exp(log_a)
        bb = jnp.sqrt(-jnp.tanh(log_a) * (a * a + 1.0)) * (i * xc)
        for s in range(nslab):
            a_ref[s, pl.ds(r0, chunk), :] = a[:, s * LANES:(s + 1) * LANES]
            b_ref[s, pl.ds(r0, chunk), :] = bb[:, s * LANES:(s + 1) * LANES]

    def step(t, hs):
        new = []
        for s in range(nslab):
            rows_t = _time_rows(t, t_steps)
            h = a_ref[s, rows_t, :] * hs[s] + b_ref[s, rows_t, :]
            b_ref[s, rows_t, :] = h
            new.append(h)
        return tuple(new)

    hs = lax.fori_loop(0, t_steps, step, tuple(h_ref[s] for s in range(nslab)), unroll=8)
    for s in range(nslab):
        h_ref[s] = hs[s]

    for b in range(SCAN_BATCH):
        r0 = _seq_row0(b, t_steps)
        for s in range(nslab):
            h = b_ref[s, pl.ds(r0, t_steps), :]
            gt = gate_ref[b, :, s * LANES:(s + 1) * LANES]
            o_ref[b, :, s * LANES:(s + 1) * LANES] = (jax.nn.gelu(gt) * h).astype(o_ref.dtype)


def rglru(xr, gate, conv_w, conv_b, w_gates, gate_b, lru_lam, t_steps=None):
    bsz, seq, width = xr.shape
    t_steps = t_steps or min(LRU_T, seq)
    rows = _rows(t_steps)
    row_chunks = next(c for c in (5, 4, 3, 2, 1) if (rows // SUBLANES) % c == 0)
    blk = pl.BlockSpec((SCAN_BATCH, t_steps, width), lambda g, t: (g, t, 0))
    return pl.pallas_call(
        functools.partial(_rglru_kernel, t_steps=t_steps, row_chunks=row_chunks),
        grid=(bsz // SCAN_BATCH, seq // t_steps),
        in_specs=[blk, blk, _const_spec(conv_w.shape), _const_spec(conv_b.shape),
                  _const_spec(w_gates.shape), _const_spec(gate_b.shape),
                  _const_spec(lru_lam.shape)],
        out_specs=blk,
        out_shape=jax.ShapeDtypeStruct((bsz, seq, width), BF16),
        scratch_shapes=[pltpu.VMEM((CONV_HEAD + rows, width), F32),
                        pltpu.VMEM((width // LANES, rows, LANES), F32),
                        pltpu.VMEM((width // LANES, rows, LANES), F32),
                        pltpu.VMEM((width // LANES, SCAN_BATCH, LANES), F32)],
        compiler_params=_cparams(("parallel", "arbitrary")),
        name="rglru",
    )(xr, gate, conv_w, conv_b, w_gates, gate_b, lru_lam)


S5_PAIRS = S5_GROUPS // 2
S5_PAIRS_PER_PASS = 4
S5_PAIRS_PER_BLOCK = LANES // (2 * S5_GROUP)


def _s5_kernel(u_ref, wb_ref, are_ref, aim_ref, wc_ref, d_ref, wglu_ref, bglu_ref, o_ref,
               upad_ref, xs_ref, h_ref, y_ref, *, t_steps):
    ti = pl.program_id(1)
    width = u_ref.shape[2]
    ppp = S5_PAIRS_PER_PASS
    n_pass = S5_PAIRS // ppp

    @pl.when(ti == 0)
    def _():
        upad_ref[...] = jnp.zeros_like(upad_ref)
        h_ref[...] = jnp.zeros_like(h_ref)

    for b in range(SCAN_BATCH):
        upad_ref[pl.ds(_seq_row0(b, t_steps), t_steps), :] = u_ref[b]

    def b_proj(p):
        blk = p // S5_PAIRS_PER_BLOCK
        ub = upad_ref[:, blk * LANES:(blk + 1) * LANES].astype(BF16)
        x = jnp.dot(ub, wb_ref[p], preferred_element_type=F32)
        xs_ref[2 * p] = x[:, :LANES]
        xs_ref[2 * p + 1] = x[:, LANES:]

    def c_proj(p):
        q, first, last = p // ppp, p % ppp == 0, p % ppp == ppp - 1
        lanes = slice(q * LANES, (q + 1) * LANES)
        hcat = jnp.concatenate([xs_ref[2 * p], xs_ref[2 * p + 1]], axis=-1).astype(BF16)
        y = jnp.dot(hcat, wc_ref[p], preferred_element_type=F32)
        y = y + (d_ref[:, lanes] * upad_ref[:, lanes] if first else y_ref[:, lanes])
        y_ref[:, lanes] = jax.nn.gelu(y) if last else y

    def scan(pairs, a_re, a_im, hs, t0, t1):
        for t in range(t0, t1):
            rows_t = _time_rows(t, t_steps)
            new = []
            for n, p in enumerate(pairs):
                h_re, h_im = hs[2 * n], hs[2 * n + 1]
                n_re = a_re[n] * h_re - a_im[n] * h_im + xs_ref[2 * p, rows_t, :]
                n_im = a_re[n] * h_im + a_im[n] * h_re + xs_ref[2 * p + 1, rows_t, :]
                xs_ref[2 * p, rows_t, :] = n_re
                xs_ref[2 * p + 1, rows_t, :] = n_im
                new += [n_re, n_im]
            hs = tuple(new)
        return hs

    for p in range(ppp):
        b_proj(p)
    for k in range(n_pass):
        pairs = range(k * ppp, (k + 1) * ppp)
        a_re = [jnp.broadcast_to(are_ref[p:p + 1, :], (SCAN_BATCH, LANES)) for p in pairs]
        a_im = [jnp.broadcast_to(aim_ref[p:p + 1, :], (SCAN_BATCH, LANES)) for p in pairs]
        hs = tuple(h_ref[s] for p in pairs for s in (2 * p, 2 * p + 1))
        side = []
        if k + 1 < n_pass:
            side += [functools.partial(b_proj, p + ppp) for p in pairs]
        if k > 0:
            side += [functools.partial(c_proj, p - ppp) for p in pairs]
        seg = t_steps // len(side)
        for i, matmul in enumerate(side):
            matmul()
            hs = scan(pairs, a_re, a_im, hs, i * seg, t_steps if i + 1 == len(side) else (i + 1) * seg)
        for n, p in enumerate(pairs):
            h_ref[2 * p] = hs[2 * n]
            h_ref[2 * p + 1] = hs[2 * n + 1]
    for p in range((n_pass - 1) * ppp, n_pass * ppp):
        c_proj(p)

    z = y_ref[...]
    gate = jnp.dot(z.astype(BF16), wglu_ref[...], preferred_element_type=F32) + bglu_ref[...]
    y_ref[...] = z * _sigmoid(gate)
    for b in range(SCAN_BATCH):
        o_ref[b] = y_ref[pl.ds(_seq_row0(b, t_steps), t_steps), :].astype(o_ref.dtype)


def s5(u, wb, a_re, a_im, wc, d_skip, w_glu, b_glu, t_steps=None):
    bsz, seq, width = u.shape
    t_steps = t_steps or min(S5_T, seq)
    rows = _rows(t_steps)
    blk = pl.BlockSpec((SCAN_BATCH, t_steps, width), lambda g, t: (g, t, 0))
    nslab = 2 * S5_PAIRS
    return pl.pallas_call(
        functools.partial(_s5_kernel, t_steps=t_steps),
        grid=(bsz // SCAN_BATCH, seq // t_steps),
        in_specs=[blk, _const_spec(wb.shape), _const_spec(a_re.shape), _const_spec(a_im.shape),
                  _const_spec(wc.shape), _const_spec(d_skip.shape), _const_spec(w_glu.shape),
                  _const_spec(b_glu.shape)],
        out_specs=blk,
        out_shape=jax.ShapeDtypeStruct((bsz, seq, width), BF16),
        scratch_shapes=[pltpu.VMEM((rows, width), F32),
                        pltpu.VMEM((nslab, rows, LANES), F32),
                        pltpu.VMEM((nslab, SCAN_BATCH, LANES), F32),
                        pltpu.VMEM((rows, width), F32)],
        compiler_params=_cparams(("parallel", "arbitrary")),
        name="s5",
    )(u, wb, a_re, a_im, wc, d_skip, w_glu, b_glu)


def _s5_params(lam_re, lam_im, log_step, b_re, b_im, c_re, c_im):
    step = jnp.exp(log_step.astype(F32))[:, None]
    lr = jnp.minimum(lam_re.astype(F32), -1e-4)
    li = lam_im.astype(F32)
    mag = jnp.exp(lr * step)
    ang = li * step
    ab_re, ab_im = mag * jnp.cos(ang), mag * jnp.sin(ang)
    den = lr * lr + li * li
    nr, ni = ab_re - 1.0, ab_im
    f_re = (nr * lr + ni * li) / den
    f_im = (ni * lr - nr * li) / den
    bb_re = f_re[:, :, None] * b_re - f_im[:, :, None] * b_im
    bb_im = f_re[:, :, None] * b_im + f_im[:, :, None] * b_re
    g, n, c = bb_re.shape
    eye2 = jnp.eye(2, dtype=F32)
    eye_b = jnp.eye(S5_PAIRS_PER_BLOCK, dtype=F32)

    def in_mat(bb):
        bt = bb.transpose(0, 2, 1).reshape(S5_PAIRS, 2, c, n)
        m = jnp.einsum('pgcn,gh->pgchn', bt, eye2).reshape(S5_PAIRS, 2 * c, 2 * n)
        slot = jnp.arange(S5_PAIRS) % S5_PAIRS_PER_BLOCK
        sel = eye_b[slot]
        return jnp.einsum('pkn,ps->pskn', m, sel).reshape(S5_PAIRS, LANES, 2 * n)

    wb = jnp.concatenate([in_mat(bb_re), in_mat(bb_im)], axis=-1).astype(BF16)

    pairs_per_out = S5_PAIRS // (HALF_MIX // LANES)
    eye_o = jnp.eye(pairs_per_out, dtype=F32)

    def out_mat(cc):
        ct = cc.transpose(0, 2, 1).reshape(S5_PAIRS, 2, n, c)
        m = jnp.einsum('pgnc,gh->pgnhc', ct, eye2).reshape(S5_PAIRS, 2 * n, 2 * c)
        slot = jnp.arange(S5_PAIRS) % pairs_per_out
        sel = eye_o[slot]
        return jnp.einsum('pnk,ps->pnsk', m, sel).reshape(S5_PAIRS, 2 * n, LANES)

    wc = jnp.concatenate([out_mat(c_re.astype(F32)), -out_mat(c_im.astype(F32))],
                         axis=1).astype(BF16)
    a_re = ab_re.reshape(S5_PAIRS, 2 * n)
    a_im = ab_im.reshape(S5_PAIRS, 2 * n)
    return wb, a_re, a_im, wc


def _retention_kernel(q_ref, k_ref, v_ref, g_ref, ng_ref, din_ref, xi_ref, zeta_ref, mask_ref,
                      cd_ref, o_ref):
    seq = q_ref.shape[0]
    c = RET_CHUNK
    nc = seq // c
    heads = LANES // RET_QK
    r = [None] * heads
    for ci in range(nc):
        rows = slice(ci * c, (ci + 1) * c)
        qb = q_ref[rows, :]
        kf = k_ref[rows, :].astype(F32)
        for hh in range(heads):
            lanes = slice(hh * RET_V, (hh + 1) * RET_V)
            vb = v_ref[rows, lanes]
            inner = lax.dot_general(qb, (kf * mask_ref[hh]).astype(BF16), (((1,), (1,)), ((), ())),
                                    preferred_element_type=F32) * din_ref[hh]
            o = jnp.dot(inner.astype(BF16), vb, preferred_element_type=F32)
            if r[hh] is not None:
                o = o + jnp.dot(qb, r[hh].astype(BF16), preferred_element_type=F32) * xi_ref[hh]
            if ci + 1 < nc:
                kz = (kf * zeta_ref[hh]).astype(BF16)
                upd = lax.dot_general(kz, vb, (((0,), (0,)), ((), ())),
                                      preferred_element_type=F32)
                r[hh] = upd if r[hh] is None else cd_ref[hh] * r[hh] + upd
            mu = jnp.mean(o, axis=-1, keepdims=True)
            oc = o - mu
            var = jnp.mean(oc * oc, axis=-1, keepdims=True)
            on = oc * lax.rsqrt(var + LN_EPS) * ng_ref[...]
            gt = g_ref[rows, lanes]
            o_ref[rows, lanes] = (gt * _sigmoid(gt) * on).astype(o_ref.dtype)


def retention(q, k, v, g, norm_g, tables):
    bsz, seq, _ = q.shape
    width = v.shape[2]
    heads = LANES // RET_QK
    pairs = width // (heads * RET_V)
    qk_spec = pl.BlockSpec((None, seq, LANES), lambda b, p: (b, 0, p))
    v_spec = pl.BlockSpec((None, seq, heads * RET_V), lambda b, p: (b, 0, p))
    tables = [t.reshape((pairs, heads) + t.shape[1:]) for t in tables]
    tab_specs = [pl.BlockSpec((None,) + t.shape[1:], lambda b, p: (p, 0, 0, 0)) for t in tables]
    return pl.pallas_call(
        _retention_kernel,
        grid=(bsz, pairs),
        in_specs=[qk_spec, qk_spec, v_spec, v_spec, _const_spec(norm_g.shape)] + tab_specs,
        out_specs=v_spec,
        out_shape=jax.ShapeDtypeStruct((bsz, seq, width), BF16),
        compiler_params=_cparams(("parallel", "parallel")),
        name="retention",
    )(q, k, v, g, norm_g, *tables)


def _retention_tables():
    c = RET_CHUNK
    log_g = jnp.log(1.0 - jnp.exp2(-5.0 - jnp.arange(RET_HEADS, dtype=F32)))
    idx = jnp.arange(c, dtype=F32)
    diff = idx[:, None] - idx[None, :]
    causal = diff >= 0
    decay_in = jnp.where(causal[None], jnp.exp(log_g[:, None, None] * jnp.where(causal, diff, 0.0)[None]), 0.0)
    xi = jnp.exp(log_g[:, None] * (idx + 1.0))
    zeta = jnp.exp(log_g[:, None] * (c - 1.0 - idx))
    chunk_decay = jnp.exp(log_g * c)
    bc = lambda t: jnp.broadcast_to(t[:, :, None], (RET_HEADS, c, LANES))
    cd = jnp.broadcast_to(chunk_decay[:, None, None], (RET_HEADS, 1, LANES))
    lane_head = jnp.arange(LANES) // RET_QK
    mask = (lane_head[None, :] == (jnp.arange(RET_HEADS) % (LANES // RET_QK))[:, None]).astype(F32)
    mask = mask[:, None, :]
    return decay_in, bc(xi), bc(zeta) * mask, mask, cd


def _kv_proj_kernel(m_ref, w_ref, k_ref, v_ref):
    mb = m_ref[...].astype(BF16)
    d = k_ref.shape[1]
    k_ref[...] = jnp.dot(mb, w_ref[:, :d], preferred_element_type=F32).astype(k_ref.dtype)
    v_ref[...] = jnp.dot(mb, w_ref[:, d:], preferred_element_type=F32).astype(v_ref.dtype)


def kv_proj(mem, w_all):
    n, d = mem.shape
    depth = w_all.shape[0]
    out_spec = pl.BlockSpec((None, TM, d), lambda l, i: (l, i, 0))
    return pl.pallas_call(
        _kv_proj_kernel,
        grid=(depth, n // TM),
        in_specs=[pl.BlockSpec((TM, d), lambda l, i: (i, 0)),
                  pl.BlockSpec((None,) + w_all.shape[1:], lambda l, i: (l, 0, 0))],
        out_specs=[out_spec, out_spec],
        out_shape=[jax.ShapeDtypeStruct((depth, n, d), BF16)] * 2,
        compiler_params=_cparams(("parallel", "parallel")),
        name="kv_proj",
    )(mem, w_all)


def _mix_out_xattn_kernel(x_ref, a_ref, b_ref, wm_ref, k_ref, v_ref, wq_ref, wo_ref,
                          g1_ref, b1_ref, g2_ref, b2_ref, o_ref, *scratch):
    half = a_ref.shape[1]
    hrows = x_ref.shape[0] // 2

    def stages(r):
        rows = slice(r * hrows, (r + 1) * hrows)
        y1_ref, x1_ref, y2_ref, q_ref, oh_ref = scratch[5 * r:5 * r + 5]

        def mix_out():
            m = jnp.dot(a_ref[rows, :], wm_ref[:half, :], preferred_element_type=F32)
            m = m + jnp.dot(b_ref[rows, :], wm_ref[half:, :], preferred_element_type=F32)
            y1_ref[...] = DN_ALPHA * x_ref[rows, :] + m

        def norm1():
            x1_ref[...] = _layer_norm(y1_ref[...], g1_ref[...], b1_ref[...])

        def q_proj():
            q = jnp.dot(x1_ref[...].astype(BF16), wq_ref[...], preferred_element_type=F32)
            q_ref[...] = (q * (XA_HEAD ** -0.5)).astype(BF16)

        def attend():
            for h in range(XA_HEADS):
                cols = slice(h * XA_HEAD, (h + 1) * XA_HEAD)
                s = lax.dot_general(q_ref[:, cols], k_ref[:, cols], (((1,), (1,)), ((), ())),
                                    preferred_element_type=F32)
                s = s - jnp.max(s, axis=-1, keepdims=True)
                p = jnp.exp(s)
                p = p / jnp.sum(p, axis=-1, keepdims=True)
                oh_ref[:, cols] = jnp.dot(p.astype(BF16), v_ref[:, cols],
                                          preferred_element_type=F32).astype(BF16)

        def o_proj():
            c = jnp.dot(oh_ref[...], wo_ref[...], preferred_element_type=F32)
            y2_ref[...] = DN_ALPHA * x1_ref[...] + c

        def norm2():
            o_ref[rows, :] = _layer_norm(y2_ref[...], g2_ref[...], b2_ref[...])

        return [mix_out, norm1, q_proj, attend, o_proj, norm2]

    _emit_skewed(stages(0), stages(1))


def mix_out_xattn(x, a, b, w_mix, k, v, wq_all, wo_all, layer, g1, b1, g2, b2, seq):
    n, d = x.shape
    half = a.shape[1]
    tm = min(TM, seq)
    tiles_per_seq = seq // tm
    row_spec = lambda w: pl.BlockSpec((tm, w), lambda i: (i, 0))
    kv_spec = pl.BlockSpec((None, MEM_LEN, d), lambda i: (layer, i // tiles_per_seq, 0))
    vec = _const_spec((1, d))
    return pl.pallas_call(
        _mix_out_xattn_kernel,
        grid=(n // tm,),
        in_specs=[row_spec(d), row_spec(half), row_spec(half), _const_spec(w_mix.shape),
                  kv_spec, kv_spec, _stacked_spec(wq_all, (layer,)), _stacked_spec(wo_all, (layer,)),
                  vec, vec, vec, vec],
        out_specs=row_spec(d),
        out_shape=jax.ShapeDtypeStruct((n, d), F32),
        scratch_shapes=[pltpu.VMEM((tm // 2, d), dt)
                        for _ in range(2) for dt in (F32, F32, F32, BF16, BF16)],
        compiler_params=_cparams(("parallel",)),
        name="mix_out_xattn",
    )(x, a, b, w_mix, k, v, wq_all, wo_all, g1, b1, g2, b2)


def _lru_gate_matrix(gate_w):
    eye = jnp.eye(LRU_BLOCKS, dtype=F32)
    dense = jnp.einsum('gncd,nm->gncmd', gate_w.astype(F32), eye)
    width = LRU_BLOCKS * LRU_BLOCK
    dense = dense.reshape(2, width, width)
    return jnp.concatenate([dense[0], dense[1]], axis=-1).astype(BF16)


def _diff_lambda_init(layer):
    return 0.8 - 0.6 * math.exp(-0.3 * layer)


EVEN_SECTIONS = (
    (0, HALF_MIX, True, DA_HEAD ** -0.5 * math.log2(math.e)),
    (HALF_MIX, HALF_MIX, True, 1.0),
    (2 * HALF_MIX, HALF_MIX, False, 1.0),
    (3 * HALF_MIX, HALF_MIX, False, 1.0),
    (4 * HALF_MIX, HALF_MIX, False, 1.0),
)
EVEN_DTYPES = (BF16, BF16, BF16, F32, F32)
RET_QK_W = RET_HEADS * RET_QK
ODD_SECTIONS = (
    (0, HALF_MIX, False, 1.0),
    (HALF_MIX, RET_QK_W, True, 1.0),
    (HALF_MIX + RET_QK_W, RET_QK_W, True, RET_QK ** -0.5),
    (HALF_MIX + 2 * RET_QK_W, HALF_MIX, False, 1.0),
    (2 * HALF_MIX + 2 * RET_QK_W, HALF_MIX, False, 1.0),
)
ODD_DTYPES = (F32, BF16, BF16, BF16, F32)


def kernel(x, mem, ln_g, ln_b, ffn_w_gate, ffn_w_up, ffn_w_down, xa_w_q, xa_w_kv, xa_w_o,
           ev_w_in, ev_w_out, da_lambda, da_norm_g, lru_conv_w, lru_conv_b, lru_gate_w,
           lru_gate_b, lru_lambda, od_w_in, od_w_out, s5_lam_re, s5_lam_im, s5_log_step,
           s5_b_re, s5_b_im, s5_c_re, s5_c_im, s5_d, s5_glu_w, s5_glu_b, ret_norm_g):
    bsz, seq, d = x.shape
    n = bsz * seq
    depth = ln_g.shape[0]
    mem2 = mem.reshape(bsz * mem.shape[1], d)
    da_tables = _rotary_tables(seq, DA_ROPE, ROPE_THETA, DA_HEAD)
    ret_rot_tables = _rotary_tables(seq, RET_QK, RET_THETA, RET_QK)
    ret_tables = _retention_tables()
    row = lambda v: v.reshape(1, -1).astype(F32)

    wg, wu, wd = ffn_w_gate.astype(BF16), ffn_w_up.astype(BF16), ffn_w_down.astype(BF16)
    wq_all, wo_all = xa_w_q.astype(BF16), xa_w_o.astype(BF16)
    k_all, v_all = kv_proj(mem2, xa_w_kv.astype(BF16))

    h = x.reshape(n, d)
    for l in range(depth):
        h = ffn_ln(h, wg, wu, wd, (l, 0), row(ln_g[l, 0]), row(ln_b[l, 0]))

        if l % 2 == 0:
            e = l // 2
            q, k, v, gate, xr = in_proj(h, ev_w_in[e].astype(BF16), da_tables, EVEN_SECTIONS,
                                        EVEN_DTYPES, DA_ROPE // 2, seq)
            shp = (bsz, seq, HALF_MIX)
            a_out = diff_attn(q.reshape(shp), k.reshape(shp), v.reshape(shp), da_lambda[e],
                              row(da_norm_g[e]), _diff_lambda_init(l))
            b_out = rglru(xr.reshape(shp), gate.reshape(shp), lru_conv_w[e], row(lru_conv_b[e]),
                          _lru_gate_matrix(lru_gate_w[e]), row(lru_gate_b[e]), row(lru_lambda[e]))
            w_out = ev_w_out[e]
        else:
            o = l // 2
            u, q, k, v, g = in_proj(h, od_w_in[o].astype(BF16), ret_rot_tables, ODD_SECTIONS,
                                    ODD_DTYPES, RET_QK // 2, seq)
            shp = (bsz, seq, HALF_MIX)
            wb, a_re, a_im, wc = _s5_params(s5_lam_re[o], s5_lam_im[o], s5_log_step[o],
                                            s5_b_re[o], s5_b_im[o], s5_c_re[o], s5_c_im[o])
            a_out = s5(u.reshape(shp), wb, a_re, a_im, wc, row(s5_d[o]),
                       s5_glu_w[o].astype(BF16), row(s5_glu_b[o]))
            qk_shp = (bsz, seq, RET_QK_W)
            b_out = retention(q.reshape(qk_shp), k.reshape(qk_shp), v.reshape(shp),
                              g.reshape(shp), row(ret_norm_g[o]), ret_tables)
            w_out = od_w_out[o]
        h = mix_out_xattn(h, a_out.reshape(n, HALF_MIX), b_out.reshape(n, HALF_MIX),
                          w_out.astype(BF16), k_all, v_all, wq_all, wo_all, l,
                          row(ln_g[l, 1]), row(ln_b[l, 1]), row(ln_g[l, 2]), row(ln_b[l, 2]), seq)
        h = ffn_ln(h, wg, wu, wd, (l, 1), row(ln_g[l, 3]), row(ln_b[l, 3]))
    return h.reshape(bsz, seq, d)
```

```python
import functools
import math

import jax
import jax.numpy as jnp
from jax import lax
from jax.experimental import pallas as pl
from jax.experimental.pallas import tpu as pltpu

F32 = jnp.float32
BF16 = jnp.bfloat16

DEPTH = 4
HALF_MIX = 512
DA_HEAD = 64
DA_ROPE = 16
ROPE_THETA = 500000.0
LRU_BLOCKS = 8
LRU_BLOCK = 64
CONV_W = 4
LRU_C = 8.0
S5_GROUP = 16
S5_GROUPS = 32
RET_HEADS = 4
RET_QK = 64
RET_V = 128
RET_THETA = 10000.0
RET_CHUNK = 128
MEM_LEN = 256
XA_HEADS = 4
XA_HEAD = 256
DN_ALPHA = (2 * DEPTH) ** 0.25
LN_EPS = 1e-5

LANES = 128
SUBLANES = 8
VMEM_LIMIT = 56 * 1024 * 1024

TM = 1024
FF_CHUNK = 256
TQ = 256
SCAN_BATCH = SUBLANES
LRU_T = 256
S5_T = 128
PITCH_PAD = 4
NEG_BIG = -1e30


def _cparams(sem):
    return pltpu.CompilerParams(dimension_semantics=sem, vmem_limit_bytes=VMEM_LIMIT)


def _const_spec(shape):
    nd = len(shape)
    return pl.BlockSpec(shape, lambda *_: (0,) * nd, pipeline_mode=pl.Buffered(1))


def _stacked_spec(arr, idx):
    tail = arr.shape[len(idx):]
    return pl.BlockSpec((None,) * len(idx) + tail, lambda *_: tuple(idx) + (0,) * len(tail),
                        pipeline_mode=pl.Buffered(1))


def _emit_skewed(first, second):
    first[0]()
    for i in range(1, len(first)):
        second[i - 1]()
        first[i]()
    second[-1]()


def _sigmoid(x):
    return 0.5 * jnp.tanh(0.5 * x) + 0.5


def _layer_norm(y, g, b):
    mu = jnp.mean(y, axis=-1, keepdims=True)
    yc = y - mu
    var = jnp.mean(yc * yc, axis=-1, keepdims=True)
    return yc * lax.rsqrt(var + LN_EPS) * g + b


def _ffn_kernel(x_ref, wg_ref, wu_ref, wd_ref, g_ref, b_ref, o_ref, xb_ref, a_ref):
    xb_ref[...] = x_ref[...].astype(BF16)
    dff = wg_ref.shape[1]
    for c0 in range(0, dff, FF_CHUNK):
        cols = slice(c0, c0 + FF_CHUNK)
        xb = xb_ref[...]
        g = jnp.dot(xb, wg_ref[:, cols], preferred_element_type=F32)
        u = jnp.dot(xb, wu_ref[:, cols], preferred_element_type=F32)
        a_ref[:, cols] = (g * jax.nn.sigmoid(g) * u).astype(BF16)
    half = x_ref.shape[0] // 2
    for r in range(2):
        rows = slice(r * half, (r + 1) * half)
        acc = jnp.dot(a_ref[rows, :], wd_ref[...], preferred_element_type=F32)
        y = DN_ALPHA * x_ref[rows, :] + 0.5 * acc
        o_ref[rows, :] = _layer_norm(y, g_ref[...], b_ref[...])


def ffn_ln(x, wg, wu, wd, idx, g, b):
    n, d = x.shape
    dff = wg.shape[-1]
    tm = min(TM, n)
    return pl.pallas_call(
        _ffn_kernel,
        grid=(n // tm,),
        in_specs=[
            pl.BlockSpec((tm, d), lambda i: (i, 0)),
            _stacked_spec(wg, idx),
            _stacked_spec(wu, idx),
            _stacked_spec(wd, idx),
            _const_spec((1, d)),
            _const_spec((1, d)),
        ],
        out_specs=pl.BlockSpec((tm, d), lambda i: (i, 0)),
        out_shape=jax.ShapeDtypeStruct((n, d), F32),
        scratch_shapes=[pltpu.VMEM((tm, d), BF16), pltpu.VMEM((tm, dff), BF16)],
        compiler_params=_cparams(("parallel",)),
        name="ffn_ln",
    )(x, wg, wu, wd, g, b)


def _in_proj_kernel(x_ref, w_ref, cos_ref, sa_ref, sb_ref, *o_refs, sections, rot_shift):
    xb = x_ref[...].astype(BF16)
    for (start, width, rotate, scale), o_ref in zip(sections, o_refs):
        z = jnp.dot(xb, w_ref[:, start:start + width], preferred_element_type=F32)
        if rotate:
            cos, sa, sb = cos_ref[...], sa_ref[...], sb_ref[...]
            for j in range(width // LANES):
                zj = z[:, j * LANES:(j + 1) * LANES]
                zr = (zj * cos + pltpu.roll(zj, LANES - rot_shift, 1) * sa
                      + pltpu.roll(zj, rot_shift, 1) * sb)
                o_ref[:, j * LANES:(j + 1) * LANES] = (zr * scale).astype(o_ref.dtype)
        else:
            o_ref[...] = (z * scale).astype(o_ref.dtype)


def in_proj(x, w, tables, sections, out_dtypes, rot_shift, seq):
    n, d = x.shape
    tm = min(TM, seq)
    tiles_per_seq = seq // tm
    tab_spec = pl.BlockSpec((tm, LANES), lambda i: (i % tiles_per_seq, 0))
    return pl.pallas_call(
        functools.partial(_in_proj_kernel, sections=sections, rot_shift=rot_shift),
        grid=(n // tm,),
        in_specs=[pl.BlockSpec((tm, d), lambda i: (i, 0)), _const_spec(w.shape),
                  tab_spec, tab_spec, tab_spec],
        out_specs=[pl.BlockSpec((tm, s[1]), lambda i: (i, 0)) for s in sections],
        out_shape=[jax.ShapeDtypeStruct((n, s[1]), dt) for s, dt in zip(sections, out_dtypes)],
        compiler_params=_cparams(("parallel",)),
        name="in_proj",
    )(x, w, *tables)


def _rotary_tables(seq, rot_dim, theta, group):
    half = rot_dim // 2
    inv = theta ** (-jnp.arange(half, dtype=F32) * 2.0 / rot_dim)
    ang = jnp.arange(seq, dtype=jnp.int32).astype(F32)[:, None] * inv[None, :]
    cos, sin = jnp.cos(ang), jnp.sin(ang)
    zeros = jnp.zeros((seq, group - rot_dim), F32)
    zh = jnp.zeros((seq, half), F32)
    cos_g = jnp.concatenate([cos, cos, jnp.ones((seq, group - rot_dim), F32)], -1)
    sa_g = jnp.concatenate([-sin, zh, zeros], -1)
    sb_g = jnp.concatenate([zh, sin, zeros], -1)
    reps = LANES // group
    return tuple(jnp.tile(t, (1, reps)) for t in (cos_g, sa_g, sb_g))


def _diff_attn_kernel(q_ref, k_ref, v_ref, lam_ref, g_ref, o_ref, vx_ref, s0_ref, s1_ref,
                      w0_ref, w1_ref, *, lambda_init):
    seq = q_ref.shape[0]
    tq = TQ
    vx_ref[:, :LANES] = v_ref[...]
    vx_ref[:, LANES:] = jnp.ones((seq, LANES), BF16)
    lam = lam_ref[...]
    lmbda = (jnp.exp(jnp.sum(lam[0:1] * lam[1:2], axis=-1, keepdims=True))
             - jnp.exp(jnp.sum(lam[2:3] * lam[3:4], axis=-1, keepdims=True)) + lambda_init)
    lane = lax.broadcasted_iota(jnp.int32, (tq, LANES), 1)
    row = lax.broadcasted_iota(jnp.int32, (2 * tq, tq), 0)
    col = lax.broadcasted_iota(jnp.int32, (2 * tq, tq), 1)
    causal = col <= jnp.where(row >= tq, row - tq, row)

    s_bufs, w_bufs = (s0_ref, s1_ref), (w0_ref, w1_ref)

    def scores(qi, s_buf):
        qf = q_ref[qi * tq:(qi + 1) * tq, :].astype(F32)
        qq = jnp.concatenate([jnp.where(lane < DA_HEAD, qf, 0.0),
                              jnp.where(lane >= DA_HEAD, qf, 0.0)], axis=0).astype(BF16)
        m_run = None
        for j in range(qi + 1):
            s = lax.dot_general(qq, k_ref[j * tq:(j + 1) * tq, :], (((1,), (1,)), ((), ())),
                                preferred_element_type=F32)
            if j == qi:
                s = jnp.where(causal, s, NEG_BIG)
            s_buf[:, j * tq:(j + 1) * tq] = s
            for c in range(tq // LANES):
                sc = s[:, c * LANES:(c + 1) * LANES]
                m_run = sc if m_run is None else jnp.maximum(m_run, sc)
        return jnp.max(m_run, axis=-1, keepdims=True)

    def finish(qi, m, s_buf, p_buf):
        for j in range(qi + 1):
            cols = slice(j * tq, (j + 1) * tq)
            p_buf[:, cols] = jnp.exp2(s_buf[:, cols] - m).astype(BF16)
        kv = (qi + 1) * tq
        pv = jnp.dot(p_buf[:, :kv], vx_ref[:kv, :], preferred_element_type=F32)
        o = pv[:, :LANES] / pv[:, LANES:]
        d = o[:tq] - lmbda * o[tq:]
        rms = lax.rsqrt(jnp.mean(d * d, axis=-1, keepdims=True) + LN_EPS)
        o_ref[qi * tq:(qi + 1) * tq, :] = (
            d * rms * g_ref[...] * (1.0 - lambda_init)).astype(o_ref.dtype)

    order = list(reversed(range(seq // tq)))
    pending = None
    for n, qi in enumerate(order):
        m = scores(qi, s_bufs[n % 2])
        if pending is not None:
            finish(*pending)
        pending = (qi, m, s_bufs[n % 2], w_bufs[n % 2])
    finish(*pending)


def diff_attn(q, k, v, lam, norm_g, lambda_init):
    bsz, seq, width = q.shape
    heads = width // LANES
    spec = pl.BlockSpec((None, seq, LANES), lambda b, h: (b, 0, h))
    return pl.pallas_call(
        functools.partial(_diff_attn_kernel, lambda_init=lambda_init),
        grid=(bsz, heads),
        in_specs=[spec, spec, spec, _const_spec(lam.shape), _const_spec(norm_g.shape)],
        out_specs=spec,
        out_shape=jax.ShapeDtypeStruct((bsz, seq, width), BF16),
        scratch_shapes=[pltpu.VMEM((seq, 2 * LANES), BF16),
                        pltpu.VMEM((2 * TQ, seq), F32), pltpu.VMEM((2 * TQ, seq), F32),
                        pltpu.VMEM((2 * TQ, seq), BF16), pltpu.VMEM((2 * TQ, seq), BF16)],
        compiler_params=_cparams(("parallel", "parallel")),
        name="diff_attn",
    )(q, k, v, lam, norm_g)


def _pitch(t_steps):
    return t_steps + PITCH_PAD


def _rows(t_steps):
    return SCAN_BATCH * _pitch(t_steps)


def _seq_row0(b, t_steps):
    return b * _pitch(t_steps) + PITCH_PAD


def _time_rows(t, t_steps):
    return pl.ds(PITCH_PAD + t, SCAN_BATCH, stride=_pitch(t_steps))


CONV_HEAD = SUBLANES


def _rglru_kernel(xr_ref, gate_ref, cw_ref, cb_ref, wg_ref, gb_ref, lam_ref, o_ref,
                  xpad_ref, a_ref, b_ref, h_ref, *, t_steps, row_chunks):
    ti = pl.program_id(1)
    rows = _rows(t_steps)
    width = xr_ref.shape[2]
    nslab = width // LANES
    hist = CONV_W - 1

    @pl.when(ti == 0)
    def _():
        xpad_ref[...] = jnp.zeros_like(xpad_ref)
        h_ref[...] = jnp.zeros_like(h_ref)

    @pl.when(ti > 0)
    def _():
        for b in range(SCAN_BATCH):
            r0 = CONV_HEAD + _seq_row0(b, t_steps)
            xpad_ref[pl.ds(r0 - hist, hist), :] = xpad_ref[pl.ds(r0 + t_steps - hist, hist), :]

    for b in range(SCAN_BATCH):
        xpad_ref[pl.ds(CONV_HEAD + _seq_row0(b, t_steps), t_steps), :] = xr_ref[b]

    softplus = jax.nn.softplus(-lam_ref[...])
    chunk = rows // row_chunks
    for c in range(row_chunks):
        r0 = c * chunk
        xh = xpad_ref[pl.ds(r0, CONV_HEAD + chunk), :]
        xc = cb_ref[...] + cw_ref[hist:hist + 1, :] * xh[CONV_HEAD:]
        for j in range(hist):
            xc = xc + cw_ref[j:j + 1, :] * pltpu.roll(xh, hist - j, 0)[CONV_HEAD:]
        gates = jnp.dot(xc.astype(BF16), wg_ref[...], preferred_element_type=F32) + gb_ref[...]
        r = _sigmoid(gates[:, :width])
        i = _sigmoid(gates[:, width:])
        log_a = -LRU_C * r * softplus
        a = jnp.exp(log_a)
        bb = jnp.sqrt(-jnp.tanh(log_a) * (a * a + 1.0)) * (i * xc)
        for s in range(nslab):
            a_ref[s, pl.ds(r0, chunk), :] = a[:, s * LANES:(s + 1) * LANES]
            b_ref[s, pl.ds(r0, chunk), :] = bb[:, s * LANES:(s + 1) * LANES]

    def step(t, hs):
        new = []
        for s in range(nslab):
            rows_t = _time_rows(t, t_steps)
            h = a_ref[s, rows_t, :] * hs[s] + b_ref[s, rows_t, :]
            b_ref[s, rows_t, :] = h
            new.append(h)
        return tuple(new)

    hs = lax.fori_loop(0, t_steps, step, tuple(h_ref[s] for s in range(nslab)), unroll=8)
    for s in range(nslab):
        h_ref[s] = hs[s]

    for b in range(SCAN_BATCH):
        r0 = _seq_row0(b, t_steps)
        for s in range(nslab):
            h = b_ref[s, pl.ds(r0, t_steps), :]
            gt = gate_ref[b, :, s * LANES:(s + 1) * LANES]
            o_ref[b, :, s * LANES:(s + 1) * LANES] = (jax.nn.gelu(gt) * h).astype(o_ref.dtype)


def rglru(xr, gate, conv_w, conv_b, w_gates, gate_b, lru_lam, t_steps=None):
    bsz, seq, width = xr.shape
    t_steps = t_steps or min(LRU_T, seq)
    rows = _rows(t_steps)
    row_chunks = next(c for c in (5, 4, 3, 2, 1) if (rows // SUBLANES) % c == 0)
    blk = pl.BlockSpec((SCAN_BATCH, t_steps, width), lambda g, t: (g, t, 0))
    return pl.pallas_call(
        functools.partial(_rglru_kernel, t_steps=t_steps, row_chunks=row_chunks),
        grid=(bsz // SCAN_BATCH, seq // t_steps),
        in_specs=[blk, blk, _const_spec(conv_w.shape), _const_spec(conv_b.shape),
                  _const_spec(w_gates.shape), _const_spec(gate_b.shape),
                  _const_spec(lru_lam.shape)],
        out_specs=blk,
        out_shape=jax.ShapeDtypeStruct((bsz, seq, width), BF16),
        scratch_shapes=[pltpu.VMEM((CONV_HEAD + rows, width), F32),
                        pltpu.VMEM((width // LANES, rows, LANES), F32),
                        pltpu.VMEM((width // LANES, rows, LANES), F32),
                        pltpu.VMEM((width // LANES, SCAN_BATCH, LANES), F32)],
        compiler_params=_cparams(("parallel", "arbitrary")),
        name="rglru",
    )(xr, gate, conv_w, conv_b, w_gates, gate_b, lru_lam)


S5_PAIRS = S5_GROUPS // 2
S5_PAIRS_PER_PASS = 4
S5_PAIRS_PER_BLOCK = LANES // (2 * S5_GROUP)


def _s5_kernel(u_ref, wb_ref, are_ref, aim_ref, wc_ref, d_ref, wglu_ref, bglu_ref, o_ref,
               upad_ref, xs_ref, h_ref, y_ref, *, t_steps):
    ti = pl.program_id(1)
    width = u_ref.shape[2]
    ppp = S5_PAIRS_PER_PASS
    n_pass = S5_PAIRS // ppp

    @pl.when(ti == 0)
    def _():
        upad_ref[...] = jnp.zeros_like(upad_ref)
        h_ref[...] = jnp.zeros_like(h_ref)

    for b in range(SCAN_BATCH):
        upad_ref[pl.ds(_seq_row0(b, t_steps), t_steps), :] = u_ref[b]

    def b_proj(p):
        blk = p // S5_PAIRS_PER_BLOCK
        ub = upad_ref[:, blk * LANES:(blk + 1) * LANES].astype(BF16)
        x = jnp.dot(ub, wb_ref[p], preferred_element_type=F32)
        xs_ref[2 * p] = x[:, :LANES]
        xs_ref[2 * p + 1] = x[:, LANES:]

    def c_proj(p):
        q, first, last = p // ppp, p % ppp == 0, p % ppp == ppp - 1
        lanes = slice(q * LANES, (q + 1) * LANES)
        hcat = jnp.concatenate([xs_ref[2 * p], xs_ref[2 * p + 1]], axis=-1).astype(BF16)
        y = jnp.dot(hcat, wc_ref[p], preferred_element_type=F32)
        y = y + (d_ref[:, lanes] * upad_ref[:, lanes] if first else y_ref[:, lanes])
        y_ref[:, lanes] = jax.nn.gelu(y) if last else y

    def scan(pairs, a_re, a_im, hs, t0, t1):
        for t in range(t0, t1):
            rows_t = _time_rows(t, t_steps)
            new = []
            for n, p in enumerate(pairs):
                h_re, h_im = hs[2 * n], hs[2 * n + 1]
                n_re = a_re[n] * h_re - a_im[n] * h_im + xs_ref[2 * p, rows_t, :]
                n_im = a_re[n] * h_im + a_im[n] * h_re + xs_ref[2 * p + 1, rows_t, :]
                xs_ref[2 * p, rows_t, :] = n_re
                xs_ref[2 * p + 1, rows_t, :] = n_im
                new += [n_re, n_im]
            hs = tuple(new)
        return hs

    for p in range(ppp):
        b_proj(p)
    for k in range(n_pass):
        pairs = range(k * ppp, (k + 1) * ppp)
        a_re = [jnp.broadcast_to(are_ref[p:p + 1, :], (SCAN_BATCH, LANES)) for p in pairs]
        a_im = [jnp.broadcast_to(aim_ref[p:p + 1, :], (SCAN_BATCH, LANES)) for p in pairs]
        hs = tuple(h_ref[s] for p in pairs for s in (2 * p, 2 * p + 1))
        side = []
        if k + 1 < n_pass:
            side += [functools.partial(b_proj, p + ppp) for p in pairs]
        if k > 0:
            side += [functools.partial(c_proj, p - ppp) for p in pairs]
        seg = t_steps // len(side)
        for i, matmul in enumerate(side):
            matmul()
            hs = scan(pairs, a_re, a_im, hs, i * seg, t_steps if i + 1 == len(side) else (i + 1) * seg)
        for n, p in enumerate(pairs):
            h_ref[2 * p] = hs[2 * n]
            h_ref[2 * p + 1] = hs[2 * n + 1]
    for p in range((n_pass - 1) * ppp, n_pass * ppp):
        c_proj(p)

    z = y_ref[...]
    gate = jnp.dot(z.astype(BF16), wglu_ref[...], preferred_element_type=F32) + bglu_ref[...]
    y_ref[...] = z * _sigmoid(gate)
    for b in range(SCAN_BATCH):
        o_ref[b] = y_ref[pl.ds(_seq_row0(b, t_steps), t_steps), :].astype(o_ref.dtype)


def s5(u, wb, a_re, a_im, wc, d_skip, w_glu, b_glu, t_steps=None):
    bsz, seq, width = u.shape
    t_steps = t_steps or min(S5_T, seq)
    rows = _rows(t_steps)
    blk = pl.BlockSpec((SCAN_BATCH, t_steps, width), lambda g, t: (g, t, 0))
    nslab = 2 * S5_PAIRS
    return pl.pallas_call(
        functools.partial(_s5_kernel, t_steps=t_steps),
        grid=(bsz // SCAN_BATCH, seq // t_steps),
        in_specs=[blk, _const_spec(wb.shape), _const_spec(a_re.shape), _const_spec(a_im.shape),
                  _const_spec(wc.shape), _const_spec(d_skip.shape), _const_spec(w_glu.shape),
                  _const_spec(b_glu.shape)],
        out_specs=blk,
        out_shape=jax.ShapeDtypeStruct((bsz, seq, width), BF16),
        scratch_shapes=[pltpu.VMEM((rows, width), F32),
                        pltpu.VMEM((nslab, rows, LANES), F32),
                        pltpu.VMEM((nslab, SCAN_BATCH, LANES), F32),
                        pltpu.VMEM((rows, width), F32)],
        compiler_params=_cparams(("parallel", "arbitrary")),
        name="s5",
    )(u, wb, a_re, a_im, wc, d_skip, w_glu, b_glu)


def _s5_params(lam_re, lam_im, log_step, b_re, b_im, c_re, c_im):
    step = jnp.exp(log_step.astype(F32))[:, None]
    lr = jnp.minimum(lam_re.astype(F32), -1e-4)
    li = lam_im.astype(F32)
    mag = jnp.exp(lr * step)
    ang = li * step
    ab_re, ab_im = mag * jnp.cos(ang), mag * jnp.sin(ang)
    den = lr * lr + li * li
    nr, ni = ab_re - 1.0, ab_im
    f_re = (nr * lr + ni * li) / den
    f_im = (ni * lr - nr * li) / den
    bb_re = f_re[:, :, None] * b_re - f_im[:, :, None] * b_im
    bb_im = f_re[:, :, None] * b_im + f_im[:, :, None] * b_re
    g, n, c = bb_re.shape
    eye2 = jnp.eye(2, dtype=F32)
    eye_b = jnp.eye(S5_PAIRS_PER_BLOCK, dtype=F32)

    def in_mat(bb):
        bt = bb.transpose(0, 2, 1).reshape(S5_PAIRS, 2, c, n)
        m = jnp.einsum('pgcn,gh->pgchn', bt, eye2).reshape(S5_PAIRS, 2 * c, 2 * n)
        slot = jnp.arange(S5_PAIRS) % S5_PAIRS_PER_BLOCK
        sel = eye_b[slot]
        return jnp.einsum('pkn,ps->pskn', m, sel).reshape(S5_PAIRS, LANES, 2 * n)

    wb = jnp.concatenate([in_mat(bb_re), in_mat(bb_im)], axis=-1).astype(BF16)

    pairs_per_out = S5_PAIRS // (HALF_MIX // LANES)
    eye_o = jnp.eye(pairs_per_out, dtype=F32)

    def out_mat(cc):
        ct = cc.transpose(0, 2, 1).reshape(S5_PAIRS, 2, n, c)
        m = jnp.einsum('pgnc,gh->pgnhc', ct, eye2).reshape(S5_PAIRS, 2 * n, 2 * c)
        slot = jnp.arange(S5_PAIRS) % pairs_per_out
        sel = eye_o[slot]
        return jnp.einsum('pnk,ps->pnsk', m, sel).reshape(S5_PAIRS, 2 * n, LANES)

    wc = jnp.concatenate([out_mat(c_re.astype(F32)), -out_mat(c_im.astype(F32))],
                         axis=1).astype(BF16)
    a_re = ab_re.reshape(S5_PAIRS, 2 * n)
    a_im = ab_im.reshape(S5_PAIRS, 2 * n)
    return wb, a_re, a_im, wc


def _retention_kernel(q_ref, k_ref, v_ref, g_ref, ng_ref, din_ref, xi_ref, zeta_ref, mask_ref,
                      cd_ref, o_ref):
    seq = q_ref.shape[0]
    c = RET_CHUNK
    nc = seq // c
    heads = LANES // RET_QK
    r = [None] * heads
    for ci in range(nc):
        rows = slice(ci * c, (ci + 1) * c)
        qb = q_ref[rows, :]
        kf = k_ref[rows, :].astype(F32)
        for hh in range(heads):
            lanes = slice(hh * RET_V, (hh + 1) * RET_V)
            vb = v_ref[rows, lanes]
            inner = lax.dot_general(qb, (kf * mask_ref[hh]).astype(BF16), (((1,), (1,)), ((), ())),
                                    preferred_element_type=F32) * din_ref[hh]
            o = jnp.dot(inner.astype(BF16), vb, preferred_element_type=F32)
            if r[hh] is not None:
                o = o + jnp.dot(qb, r[hh].astype(BF16), preferred_element_type=F32) * xi_ref[hh]
            if ci + 1 < nc:
                kz = (kf * zeta_ref[hh]).astype(BF16)
                upd = lax.dot_general(kz, vb, (((0,), (0,)), ((), ())),
                                      preferred_element_type=F32)
                r[hh] = upd if r[hh] is None else cd_ref[hh] * r[hh] + upd
            mu = jnp.mean(o, axis=-1, keepdims=True)
            oc = o - mu
            var = jnp.mean(oc * oc, axis=-1, keepdims=True)
            on = oc * lax.rsqrt(var + LN_EPS) * ng_ref[...]
            gt = g_ref[rows, lanes]
            o_ref[rows, lanes] = (gt * _sigmoid(gt) * on).astype(o_ref.dtype)


def retention(q, k, v, g, norm_g, tables):
    bsz, seq, _ = q.shape
    width = v.shape[2]
    heads = LANES // RET_QK
    pairs = width // (heads * RET_V)
    qk_spec = pl.BlockSpec((None, seq, LANES), lambda b, p: (b, 0, p))
    v_spec = pl.BlockSpec((None, seq, heads * RET_V), lambda b, p: (b, 0, p))
    tables = [t.reshape((pairs, heads) + t.shape[1:]) for t in tables]
    tab_specs = [pl.BlockSpec((None,) + t.shape[1:], lambda b, p: (p, 0, 0, 0)) for t in tables]
    return pl.pallas_call(
        _retention_kernel,
        grid=(bsz, pairs),
        in_specs=[qk_spec, qk_spec, v_spec, v_spec, _const_spec(norm_g.shape)] + tab_specs,
        out_specs=v_spec,
        out_shape=jax.ShapeDtypeStruct((bsz, seq, width), BF16),
        compiler_params=_cparams(("parallel", "parallel")),
        name="retention",
    )(q, k, v, g, norm_g, *tables)


def _retention_tables():
    c = RET_CHUNK
    log_g = jnp.log(1.0 - jnp.exp2(-5.0 - jnp.arange(RET_HEADS, dtype=F32)))
    idx = jnp.arange(c, dtype=F32)
    diff = idx[:, None] - idx[None, :]
    causal = diff >= 0
    decay_in = jnp.where(causal[None], jnp.exp(log_g[:, None, None] * jnp.where(causal, diff, 0.0)[None]), 0.0)
    xi = jnp.exp(log_g[:, None] * (idx + 1.0))
    zeta = jnp.exp(log_g[:, None] * (c - 1.0 - idx))
    chunk_decay = jnp.exp(log_g * c)
    bc = lambda t: jnp.broadcast_to(t[:, :, None], (RET_HEADS, c, LANES))
    cd = jnp.broadcast_to(chunk_decay[:, None, None], (RET_HEADS, 1, LANES))
    lane_head = jnp.arange(LANES) // RET_QK
    mask = (lane_head[None, :] == (jnp.arange(RET_HEADS) % (LANES // RET_QK))[:, None]).astype(F32)
    mask = mask[:, None, :]
    return decay_in, bc(xi), bc(zeta) * mask, mask, cd


def _kv_proj_kernel(m_ref, w_ref, k_ref, v_ref):
    mb = m_ref[...].astype(BF16)
    d = k_ref.shape[1]
    k_ref[...] = jnp.dot(mb, w_ref[:, :d], preferred_element_type=F32).astype(k_ref.dtype)
    v_ref[...] = jnp.dot(mb, w_ref[:, d:], preferred_element_type=F32).astype(v_ref.dtype)


def kv_proj(mem, w_all):
    n, d = mem.shape
    depth = w_all.shape[0]
    out_spec = pl.BlockSpec((None, TM, d), lambda l, i: (l, i, 0))
    return pl.pallas_call(
        _kv_proj_kernel,
        grid=(depth, n // TM),
        in_specs=[pl.BlockSpec((TM, d), lambda l, i: (i, 0)),
                  pl.BlockSpec((None,) + w_all.shape[1:], lambda l, i: (l, 0, 0))],
        out_specs=[out_spec, out_spec],
        out_shape=[jax.ShapeDtypeStruct((depth, n, d), BF16)] * 2,
        compiler_params=_cparams(("parallel", "parallel")),
        name="kv_proj",
    )(mem, w_all)


def _mix_out_xattn_kernel(x_ref, a_ref, b_ref, wm_ref, k_ref, v_ref, wq_ref, wo_ref,
                          g1_ref, b1_ref, g2_ref, b2_ref, o_ref, *scratch):
    half = a_ref.shape[1]
    hrows = x_ref.shape[0] // 2

    def stages(r):
        rows = slice(r * hrows, (r + 1) * hrows)
        y1_ref, x1_ref, y2_ref, q_ref, oh_ref = scratch[5 * r:5 * r + 5]

        def mix_out():
            m = jnp.dot(a_ref[rows, :], wm_ref[:half, :], preferred_element_type=F32)
            m = m + jnp.dot(b_ref[rows, :], wm_ref[half:, :], preferred_element_type=F32)
            y1_ref[...] = DN_ALPHA * x_ref[rows, :] + m

        def norm1():
            x1_ref[...] = _layer_norm(y1_ref[...], g1_ref[...], b1_ref[...])

        def q_proj():
            q = jnp.dot(x1_ref[...].astype(BF16), wq_ref[...], preferred_element_type=F32)
            q_ref[...] = (q * (XA_HEAD ** -0.5)).astype(BF16)

        def attend():
            for h in range(XA_HEADS):
                cols = slice(h * XA_HEAD, (h + 1) * XA_HEAD)
                s = lax.dot_general(q_ref[:, cols], k_ref[:, cols], (((1,), (1,)), ((), ())),
                                    preferred_element_type=F32)
                s = s - jnp.max(s, axis=-1, keepdims=True)
                p = jnp.exp(s)
                p = p / jnp.sum(p, axis=-1, keepdims=True)
                oh_ref[:, cols] = jnp.dot(p.astype(BF16), v_ref[:, cols],
                                          preferred_element_type=F32).astype(BF16)

        def o_proj():
            c = jnp.dot(oh_ref[...], wo_ref[...], preferred_element_type=F32)
            y2_ref[...] = DN_ALPHA * x1_ref[...] + c

        def norm2():
            o_ref[rows, :] = _layer_norm(y2_ref[...], g2_ref[...], b2_ref[...])

        return [mix_out, norm1, q_proj, attend, o_proj, norm2]

    _emit_skewed(stages(0), stages(1))


def mix_out_xattn(x, a, b, w_mix, k, v, wq_all, wo_all, layer, g1, b1, g2, b2, seq):
    n, d = x.shape
    half = a.shape[1]
    tm = min(TM, seq)
    tiles_per_seq = seq // tm
    row_spec = lambda w: pl.BlockSpec((tm, w), lambda i: (i, 0))
    kv_spec = pl.BlockSpec((None, MEM_LEN, d), lambda i: (layer, i // tiles_per_seq, 0))
    vec = _const_spec((1, d))
    return pl.pallas_call(
        _mix_out_xattn_kernel,
        grid=(n // tm,),
        in_specs=[row_spec(d), row_spec(half), row_spec(half), _const_spec(w_mix.shape),
                  kv_spec, kv_spec, _stacked_spec(wq_all, (layer,)), _stacked_spec(wo_all, (layer,)),
                  vec, vec, vec, vec],
        out_specs=row_spec(d),
        out_shape=jax.ShapeDtypeStruct((n, d), F32),
        scratch_shapes=[pltpu.VMEM((tm // 2, d), dt)
                        for _ in range(2) for dt in (F32, F32, F32, BF16, BF16)],
        compiler_params=_cparams(("parallel",)),
        name="mix_out_xattn",
    )(x, a, b, w_mix, k, v, wq_all, wo_all, g1, b1, g2, b2)


def _lru_gate_matrix(gate_w):
    eye = jnp.eye(LRU_BLOCKS, dtype=F32)
    dense = jnp.einsum('gncd,nm->gncmd', gate_w.astype(F32), eye)
    width = LRU_BLOCKS * LRU_BLOCK
    dense = dense.reshape(2, width, width)
    return jnp.concatenate([dense[0], dense[1]], axis=-1).astype(BF16)


def _diff_lambda_init(layer):
    return 0.8 - 0.6 * math.exp(-0.3 * layer)


EVEN_SECTIONS = (
    (0, HALF_MIX, True, DA_HEAD ** -0.5 * math.log2(math.e)),
    (HALF_MIX, HALF_MIX, True, 1.0),
    (2 * HALF_MIX, HALF_MIX, False, 1.0),
    (3 * HALF_MIX, HALF_MIX, False, 1.0),
    (4 * HALF_MIX, HALF_MIX, False, 1.0),
)
EVEN_DTYPES = (BF16, BF16, BF16, F32, F32)
RET_QK_W = RET_HEADS * RET_QK
ODD_SECTIONS = (
    (0, HALF_MIX, False, 1.0),
    (HALF_MIX, RET_QK_W, True, 1.0),
    (HALF_MIX + RET_QK_W, RET_QK_W, True, RET_QK ** -0.5),
    (HALF_MIX + 2 * RET_QK_W, HALF_MIX, False, 1.0),
    (2 * HALF_MIX + 2 * RET_QK_W, HALF_MIX, False, 1.0),
)
ODD_DTYPES = (F32, BF16, BF16, BF16, F32)


def kernel(x, mem, ln_g, ln_b, ffn_w_gate, ffn_w_up, ffn_w_down, xa_w_q, xa_w_kv, xa_w_o,
           ev_w_in, ev_w_out, da_lambda, da_norm_g, lru_conv_w, lru_conv_b, lru_gate_w,
           lru_gate_b, lru_lambda, od_w_in, od_w_out, s5_lam_re, s5_lam_im, s5_log_step,
           s5_b_re, s5_b_im, s5_c_re, s5_c_im, s5_d, s5_glu_w, s5_glu_b, ret_norm_g):
    bsz, seq, d = x.shape
    n = bsz * seq
    depth = ln_g.shape[0]
    mem2 = mem.reshape(bsz * mem.shape[1], d)
    da_tables = _rotary_tables(seq, DA_ROPE, ROPE_THETA, DA_HEAD)
    ret_rot_tables = _rotary_tables(seq, RET_QK, RET_THETA, RET_QK)
    ret_tables = _retention_tables()
    row = lambda v: v.reshape(1, -1).astype(F32)

    wg, wu, wd = ffn_w_gate.astype(BF16), ffn_w_up.astype(BF16), ffn_w_down.astype(BF16)
    wq_all, wo_all = xa_w_q.astype(BF16), xa_w_o.astype(BF16)
    k_all, v_all = kv_proj(mem2, xa_w_kv.astype(BF16))

    h = x.reshape(n, d)
    for l in range(depth):
        h = ffn_ln(h, wg, wu, wd, (l, 0), row(ln_g[l, 0]), row(ln_b[l, 0]))

        if l % 2 == 0:
            e = l // 2
            q, k, v, gate, xr = in_proj(h, ev_w_in[e].astype(BF16), da_tables, EVEN_SECTIONS,
                                        EVEN_DTYPES, DA_ROPE // 2, seq)
            shp = (bsz, seq, HALF_MIX)
            a_out = diff_attn(q.reshape(shp), k.reshape(shp), v.reshape(shp), da_lambda[e],
                              row(da_norm_g[e]), _diff_lambda_init(l))
            b_out = rglru(xr.reshape(shp), gate.reshape(shp), lru_conv_w[e], row(lru_conv_b[e]),
                          _lru_gate_matrix(lru_gate_w[e]), row(lru_gate_b[e]), row(lru_lambda[e]))
            w_out = ev_w_out[e]
        else:
            o = l // 2
            u, q, k, v, g = in_proj(h, od_w_in[o].astype(BF16), ret_rot_tables, ODD_SECTIONS,
                                    ODD_DTYPES, RET_QK // 2, seq)
            shp = (bsz, seq, HALF_MIX)
            wb, a_re, a_im, wc = _s5_params(s5_lam_re[o], s5_lam_im[o], s5_log_step[o],
                                            s5_b_re[o], s5_b_im[o], s5_c_re[o], s5_c_im[o])
            a_out = s5(u.reshape(shp), wb, a_re, a_im, wc, row(s5_d[o]),
                       s5_glu_w[o].astype(BF16), row(s5_glu_b[o]))
            qk_shp = (bsz, seq, RET_QK_W)
            b_out = retention(q.reshape(qk_shp), k.reshape(qk_shp), v.reshape(shp),
                              g.reshape(shp), row(ret_norm_g[o]), ret_tables)
            w_out = od_w_out[o]
        h = mix_out_xattn(h, a_out.reshape(n, HALF_MIX), b_out.reshape(n, HALF_MIX),
                          w_out.astype(BF16), k_all, v_all, wq_all, wo_all, l,
                          row(ln_g[l, 1]), row(ln_b[l, 1]), row(ln_g[l, 2]), row(ln_b[l, 2]), seq)
        h = ffn_ln(h, wg, wu, wd, (l, 1), row(ln_g[l, 3]), row(ln_b[l, 3]))
    return h.reshape(bsz, seq, d)
```

```python
import functools
import math

import jax
import jax.numpy as jnp
from jax import lax
from jax.experimental import pallas as pl
from jax.experimental.pallas import tpu as pltpu

F32 = jnp.float32
BF16 = jnp.bfloat16

DEPTH = 4
HALF_MIX = 512
DA_HEAD = 64
DA_ROPE = 16
ROPE_THETA = 500000.0
LRU_BLOCKS = 8
LRU_BLOCK = 64
CONV_W = 4
LRU_C = 8.0
S5_GROUP = 16
S5_GROUPS = 32
RET_HEADS = 4
RET_QK = 64
RET_V = 128
RET_THETA = 10000.0
RET_CHUNK = 128
MEM_LEN = 256
XA_HEADS = 4
XA_HEAD = 256
DN_ALPHA = (2 * DEPTH) ** 0.25
LN_EPS = 1e-5

LANES = 128
SUBLANES = 8
VMEM_LIMIT = 56 * 1024 * 1024

TM = 1024
FF_CHUNK = 256
TQ = 256
SCAN_BATCH = SUBLANES
LRU_T = 256
S5_T = 128
PITCH_PAD = 4
NEG_BIG = -1e30


def _cparams(sem):
    return pltpu.CompilerParams(dimension_semantics=sem, vmem_limit_bytes=VMEM_LIMIT)


def _const_spec(shape):
    nd = len(shape)
    return pl.BlockSpec(shape, lambda *_: (0,) * nd, pipeline_mode=pl.Buffered(1))


def _stacked_spec(arr, idx):
    tail = arr.shape[len(idx):]
    return pl.BlockSpec((None,) * len(idx) + tail, lambda *_: tuple(idx) + (0,) * len(tail),
                        pipeline_mode=pl.Buffered(1))


def _emit_skewed(first, second):
    first[0]()
    for i in range(1, len(first)):
        second[i - 1]()
        first[i]()
    second[-1]()


def _sigmoid(x):
    return 0.5 * jnp.tanh(0.5 * x) + 0.5


def _layer_norm(y, g, b):
    mu = jnp.mean(y, axis=-1, keepdims=True)
    yc = y - mu
    var = jnp.mean(yc * yc, axis=-1, keepdims=True)
    return yc * lax.rsqrt(var + LN_EPS) * g + b


def _ffn_kernel(x_ref, wg_ref, wu_ref, wd_ref, g_ref, b_ref, o_ref, xb_ref, a_ref):
    xb_ref[...] = x_ref[...].astype(BF16)
    dff = wg_ref.shape[1]
    for c0 in range(0, dff, FF_CHUNK):
        cols = slice(c0, c0 + FF_CHUNK)
        xb = xb_ref[...]
        g = jnp.dot(xb, wg_ref[:, cols], preferred_element_type=F32)
        u = jnp.dot(xb, wu_ref[:, cols], preferred_element_type=F32)
        a_ref[:, cols] = (g * jax.nn.sigmoid(g) * u).astype(BF16)
    half = x_ref.shape[0] // 2
    for r in range(2):
        rows = slice(r * half, (r + 1) * half)
        acc = jnp.dot(a_ref[rows, :], wd_ref[...], preferred_element_type=F32)
        y = DN_ALPHA * x_ref[rows, :] + 0.5 * acc
        o_ref[rows, :] = _layer_norm(y, g_ref[...], b_ref[...])


def ffn_ln(x, wg, wu, wd, idx, g, b):
    n, d = x.shape
    dff = wg.shape[-1]
    tm = min(TM, n)
    return pl.pallas_call(
        _ffn_kernel,
        grid=(n // tm,),
        in_specs=[
            pl.BlockSpec((tm, d), lambda i: (i, 0)),
            _stacked_spec(wg, idx),
            _stacked_spec(wu, idx),
            _stacked_spec(wd, idx),
            _const_spec((1, d)),
            _const_spec((1, d)),
        ],
        out_specs=pl.BlockSpec((tm, d), lambda i: (i, 0)),
        out_shape=jax.ShapeDtypeStruct((n, d), F32),
        scratch_shapes=[pltpu.VMEM((tm, d), BF16), pltpu.VMEM((tm, dff), BF16)],
        compiler_params=_cparams(("parallel",)),
        name="ffn_ln",
    )(x, wg, wu, wd, g, b)


def _in_proj_kernel(x_ref, w_ref, cos_ref, sa_ref, sb_ref, *o_refs, sections, rot_shift):
    xb = x_ref[...].astype(BF16)
    for (start, width, rotate, scale), o_ref in zip(sections, o_refs):
        z = jnp.dot(xb, w_ref[:, start:start + width], preferred_element_type=F32)
        if rotate:
            cos, sa, sb = cos_ref[...], sa_ref[...], sb_ref[...]
            for j in range(width // LANES):
                zj = z[:, j * LANES:(j + 1) * LANES]
                zr = (zj * cos + pltpu.roll(zj, LANES - rot_shift, 1) * sa
                      + pltpu.roll(zj, rot_shift, 1) * sb)
                o_ref[:, j * LANES:(j + 1) * LANES] = (zr * scale).astype(o_ref.dtype)
        else:
            o_ref[...] = (z * scale).astype(o_ref.dtype)


def in_proj(x, w, tables, sections, out_dtypes, rot_shift, seq):
    n, d = x.shape
    tm = min(TM, seq)
    tiles_per_seq = seq // tm
    tab_spec = pl.BlockSpec((tm, LANES), lambda i: (i % tiles_per_seq, 0))
    return pl.pallas_call(
        functools.partial(_in_proj_kernel, sections=sections, rot_shift=rot_shift),
        grid=(n // tm,),
        in_specs=[pl.BlockSpec((tm, d), lambda i: (i, 0)), _const_spec(w.shape),
                  tab_spec, tab_spec, tab_spec],
        out_specs=[pl.BlockSpec((tm, s[1]), lambda i: (i, 0)) for s in sections],
        out_shape=[jax.ShapeDtypeStruct((n, s[1]), dt) for s, dt in zip(sections, out_dtypes)],
        compiler_params=_cparams(("parallel",)),
        name="in_proj",
    )(x, w, *tables)


def _rotary_tables(seq, rot_dim, theta, group):
    half = rot_dim // 2
    inv = theta ** (-jnp.arange(half, dtype=F32) * 2.0 / rot_dim)
    ang = jnp.arange(seq, dtype=jnp.int32).astype(F32)[:, None] * inv[None, :]
    cos, sin = jnp.cos(ang), jnp.sin(ang)
    zeros = jnp.zeros((seq, group - rot_dim), F32)
    zh = jnp.zeros((seq, half), F32)
    cos_g = jnp.concatenate([cos, cos, jnp.ones((seq, group - rot_dim), F32)], -1)
    sa_g = jnp.concatenate([-sin, zh, zeros], -1)
    sb_g = jnp.concatenate([zh, sin, zeros], -1)
    reps = LANES // group
    return tuple(jnp.tile(t, (1, reps)) for t in (cos_g, sa_g, sb_g))


def _diff_attn_kernel(q_ref, k_ref, v_ref, lam_ref, g_ref, o_ref, vx_ref, s0_ref, s1_ref,
                      w0_ref, w1_ref, *, lambda_init):
    seq = q_ref.shape[0]
    tq = TQ
    vx_ref[:, :LANES] = v_ref[...]
    vx_ref[:, LANES:] = jnp.ones((seq, LANES), BF16)
    lam = lam_ref[...]
    lmbda = (jnp.exp(jnp.sum(lam[0:1] * lam[1:2], axis=-1, keepdims=True))
             - jnp.exp(jnp.sum(lam[2:3] * lam[3:4], axis=-1, keepdims=True)) + lambda_init)
    lane = lax.broadcasted_iota(jnp.int32, (tq, LANES), 1)
    row = lax.broadcasted_iota(jnp.int32, (2 * tq, tq), 0)
    col = lax.broadcasted_iota(jnp.int32, (2 * tq, tq), 1)
    causal = col <= jnp.where(row >= tq, row - tq, row)

    s_bufs, w_bufs = (s0_ref, s1_ref), (w0_ref, w1_ref)

    def scores(qi, s_buf):
        qf = q_ref[qi * tq:(qi + 1) * tq, :].astype(F32)
        qq = jnp.concatenate([jnp.where(lane < DA_HEAD, qf, 0.0),
                              jnp.where(lane >= DA_HEAD, qf, 0.0)], axis=0).astype(BF16)
        m_run = None
        for j in range(qi + 1):
            s = lax.dot_general(qq, k_ref[j * tq:(j + 1) * tq, :], (((1,), (1,)), ((), ())),
                                preferred_element_type=F32)
            if j == qi:
                s = jnp.where(causal, s, NEG_BIG)
            s_buf[:, j * tq:(j + 1) * tq] = s
            for c in range(tq // LANES):
                sc = s[:, c * LANES:(c + 1) * LANES]
                m_run = sc if m_run is None else jnp.maximum(m_run, sc)
        return jnp.max(m_run, axis=-1, keepdims=True)

    def finish(qi, m, s_buf, p_buf):
        for j in range(qi + 1):
            cols = slice(j * tq, (j + 1) * tq)
            p_buf[:, cols] = jnp.exp2(s_buf[:, cols] - m).astype(BF16)
        kv = (qi + 1) * tq
        pv = jnp.dot(p_buf[:, :kv], vx_ref[:kv, :], preferred_element_type=F32)
        o = pv[:, :LANES] / pv[:, LANES:]
        d = o[:tq] - lmbda * o[tq:]
        rms = lax.rsqrt(jnp.mean(d * d, axis=-1, keepdims=True) + LN_EPS)
        o_ref[qi * tq:(qi + 1) * tq, :] = (
            d * rms * g_ref[...] * (1.0 - lambda_init)).astype(o_ref.dtype)

    order = list(reversed(range(seq // tq)))
    pending = None
    for n, qi in enumerate(order):
        m = scores(qi, s_bufs[n % 2])
        if pending is not None:
            finish(*pending)
        pending = (qi, m, s_bufs[n % 2], w_bufs[n % 2])
    finish(*pending)


def diff_attn(q, k, v, lam, norm_g, lambda_init):
    bsz, seq, width = q.shape
    heads = width // LANES
    spec = pl.BlockSpec((None, seq, LANES), lambda b, h: (b, 0, h))
    return pl.pallas_call(
        functools.partial(_diff_attn_kernel, lambda_init=lambda_init),
        grid=(bsz, heads),
        in_specs=[spec, spec, spec, _const_spec(lam.shape), _const_spec(norm_g.shape)],
        out_specs=spec,
        out_shape=jax.ShapeDtypeStruct((bsz, seq, width), BF16),
        scratch_shapes=[pltpu.VMEM((seq, 2 * LANES), BF16),
                        pltpu.VMEM((2 * TQ, seq), F32), pltpu.VMEM((2 * TQ, seq), F32),
                        pltpu.VMEM((2 * TQ, seq), BF16), pltpu.VMEM((2 * TQ, seq), BF16)],
        compiler_params=_cparams(("parallel", "parallel")),
        name="diff_attn",
    )(q, k, v, lam, norm_g)


def _pitch(t_steps):
    return t_steps + PITCH_PAD


def _rows(t_steps):
    return SCAN_BATCH * _pitch(t_steps)


def _seq_row0(b, t_steps):
    return b * _pitch(t_steps) + PITCH_PAD


def _time_rows(t, t_steps):
    return pl.ds(PITCH_PAD + t, SCAN_BATCH, stride=_pitch(t_steps))


CONV_HEAD = SUBLANES


def _rglru_kernel(xr_ref, gate_ref, cw_ref, cb_ref, wg_ref, gb_ref, lam_ref, o_ref,
                  xpad_ref, a_ref, b_ref, h_ref, *, t_steps, row_chunks):
    ti = pl.program_id(1)
    rows = _rows(t_steps)
    width = xr_ref.shape[2]
    nslab = width // LANES
    hist = CONV_W - 1

    @pl.when(ti == 0)
    def _():
        xpad_ref[...] = jnp.zeros_like(xpad_ref)
        h_ref[...] = jnp.zeros_like(h_ref)

    @pl.when(ti > 0)
    def _():
        for b in range(SCAN_BATCH):
            r0 = CONV_HEAD + _seq_row0(b, t_steps)
            xpad_ref[pl.ds(r0 - hist, hist), :] = xpad_ref[pl.ds(r0 + t_steps - hist, hist), :]

    for b in range(SCAN_BATCH):
        xpad_ref[pl.ds(CONV_HEAD + _seq_row0(b, t_steps), t_steps), :] = xr_ref[b]

    softplus = jax.nn.softplus(-lam_ref[...])
    chunk = rows // row_chunks
    for c in range(row_chunks):
        r0 = c * chunk
        xh = xpad_ref[pl.ds(r0, CONV_HEAD + chunk), :]
        xc = cb_ref[...] + cw_ref[hist:hist + 1, :] * xh[CONV_HEAD:]
        for j in range(hist):
            xc = xc + cw_ref[j:j + 1, :] * pltpu.roll(xh, hist - j, 0)[CONV_HEAD:]
        gates = jnp.dot(xc.astype(BF16), wg_ref[...], preferred_element_type=F32) + gb_ref[...]
        r = _sigmoid(gates[:, :width])
        i = _sigmoid(gates[:, width:])
        log_a = -LRU_C * r * softplus
        a = jnp.exp(log_a)
        bb = jnp.sqrt(-jnp.tanh(log_a) * (a * a + 1.0)) * (i * xc)
        for s in range(nslab):
            a_ref[s, pl.ds(r0, chunk), :] = a[:, s * LANES:(s + 1) * LANES]
            b_ref[s, pl.ds(r0, chunk), :] = bb[:, s * LANES:(s + 1) * LANES]

    def step(t, hs):
        new = []
        for s in range(nslab):
            rows_t = _time_rows(t, t_steps)
            h = a_ref[s, rows_t, :] * hs[s] + b_ref[s, rows_t, :]
            b_ref[s, rows_t, :] = h
            new.append(h)
        return tuple(new)

    hs = lax.fori_loop(0, t_steps, step, tuple(h_ref[s] for s in range(nslab)), unroll=8)
    for s in range(nslab):
        h_ref[s] = hs[s]

    for b in range(SCAN_BATCH):
        r0 = _seq_row0(b, t_steps)
        for s in range(nslab):
            h = b_ref[s, pl.ds(r0, t_steps), :]
            gt = gate_ref[b, :, s * LANES:(s + 1) * LANES]
            o_ref[b, :, s * LANES:(s + 1) * LANES] = (jax.nn.gelu(gt) * h).astype(o_ref.dtype)


def rglru(xr, gate, conv_w, conv_b, w_gates, gate_b, lru_lam, t_steps=None):
    bsz, seq, width = xr.shape
    t_steps = t_steps or min(LRU_T, seq)
    rows = _rows(t_steps)
    row_chunks = next(c for c in (5, 4, 3, 2, 1) if (rows // SUBLANES) % c == 0)
    blk = pl.BlockSpec((SCAN_BATCH, t_steps, width), lambda g, t: (g, t, 0))
    return pl.pallas_call(
        functools.partial(_rglru_kernel, t_steps=t_steps, row_chunks=row_chunks),
        grid=(bsz // SCAN_BATCH, seq // t_steps),
        in_specs=[blk, blk, _const_spec(conv_w.shape), _const_spec(conv_b.shape),
                  _const_spec(w_gates.shape), _const_spec(gate_b.shape),
                  _const_spec(lru_lam.shape)],
        out_specs=blk,
        out_shape=jax.ShapeDtypeStruct((bsz, seq, width), BF16),
        scratch_shapes=[pltpu.VMEM((CONV_HEAD + rows, width), F32),
                        pltpu.VMEM((width // LANES, rows, LANES), F32),
                        pltpu.VMEM((width // LANES, rows, LANES), F32),
                        pltpu.VMEM((width // LANES, SCAN_BATCH, LANES), F32)],
        compiler_params=_cparams(("parallel", "arbitrary")),
        name="rglru",
    )(xr, gate, conv_w, conv_b, w_gates, gate_b, lru_lam)


S5_PAIRS = S5_GROUPS // 2
S5_PAIRS_PER_PASS = 4
S5_PAIRS_PER_BLOCK = LANES // (2 * S5_GROUP)


def _s5_kernel(u_ref, wb_ref, are_ref, aim_ref, wc_ref, d_ref, wglu_ref, bglu_ref, o_ref,
               upad_ref, xs_ref, h_ref, y_ref, *, t_steps):
    ti = pl.program_id(1)
    width = u_ref.shape[2]
    ppp = S5_PAIRS_PER_PASS
    n_pass = S5_PAIRS // ppp

    @pl.when(ti == 0)
    def _():
        upad_ref[...] = jnp.zeros_like(upad_ref)
        h_ref[...] = jnp.zeros_like(h_ref)

    for b in range(SCAN_BATCH):
        upad_ref[pl.ds(_seq_row0(b, t_steps), t_steps), :] = u_ref[b]

    def b_proj(p):
        blk = p // S5_PAIRS_PER_BLOCK
        ub = upad_ref[:, blk * LANES:(blk + 1) * LANES].astype(BF16)
        x = jnp.dot(ub, wb_ref[p], preferred_element_type=F32)
        xs_ref[2 * p] = x[:, :LANES]
        xs_ref[2 * p + 1] = x[:, LANES:]

    def c_proj(p):
        q, first, last = p // ppp, p % ppp == 0, p % ppp == ppp - 1
        lanes = slice(q * LANES, (q + 1) * LANES)
        hcat = jnp.concatenate([xs_ref[2 * p], xs_ref[2 * p + 1]], axis=-1).astype(BF16)
        y = jnp.dot(hcat, wc_ref[p], preferred_element_type=F32)
        y = y + (d_ref[:, lanes] * upad_ref[:, lanes] if first else y_ref[:, lanes])
        y_ref[:, lanes] = jax.nn.gelu(y) if last else y

    def scan(pairs, a_re, a_im, hs, t0, t1):
        for t in range(t0, t1):
            rows_t = _time_rows(t, t_steps)
            new = []
            for n, p in enumerate(pairs):
                h_re, h_im = hs[2 * n], hs[2 * n + 1]
                n_re = a_re[n] * h_re - a_im[n] * h_im + xs_ref[2 * p, rows_t, :]
                n_im = a_re[n] * h_im + a_im[n] * h_re + xs_ref[2 * p + 1, rows_t, :]
                xs_ref[2 * p, rows_t, :] = n_re
                xs_ref[2 * p + 1, rows_t, :] = n_im
                new += [n_re, n_im]
            hs = tuple(new)
        return hs

    for p in range(ppp):
        b_proj(p)
    for k in range(n_pass):
        pairs = range(k * ppp, (k + 1) * ppp)
        a_re = [jnp.broadcast_to(are_ref[p:p + 1, :], (SCAN_BATCH, LANES)) for p in pairs]
        a_im = [jnp.broadcast_to(aim_ref[p:p + 1, :], (SCAN_BATCH, LANES)) for p in pairs]
        hs = tuple(h_ref[s] for p in pairs for s in (2 * p, 2 * p + 1))
        side = []
        if k + 1 < n_pass:
            side += [functools.partial(b_proj, p + ppp) for p in pairs]
        if k > 0:
            side += [functools.partial(c_proj, p - ppp) for p in pairs]
        seg = t_steps // len(side)
        for i, matmul in enumerate(side):
            matmul()
            hs = scan(pairs, a_re, a_im, hs, i * seg, t_steps if i + 1 == len(side) else (i + 1) * seg)
        for n, p in enumerate(pairs):
            h_ref[2 * p] = hs[2 * n]
            h_ref[2 * p + 1] = hs[2 * n + 1]
    for p in range((n_pass - 1) * ppp, n_pass * ppp):
        c_proj(p)

    z = y_ref[...]
    gate = jnp.dot(z.astype(BF16), wglu_ref[...], preferred_element_type=F32) + bglu_ref[...]
    y_ref[...] = z * _sigmoid(gate)
    for b in range(SCAN_BATCH):
        o_ref[b] = y_ref[pl.ds(_seq_row0(b, t_steps), t_steps), :].astype(o_ref.dtype)


def s5(u, wb, a_re, a_im, wc, d_skip, w_glu, b_glu, t_steps=None):
    bsz, seq, width = u.shape
    t_steps = t_steps or min(S5_T, seq)
    rows = _rows(t_steps)
    blk = pl.BlockSpec((SCAN_BATCH, t_steps, width), lambda g, t: (g, t, 0))
    nslab = 2 * S5_PAIRS
    return pl.pallas_call(
        functools.partial(_s5_kernel, t_steps=t_steps),
        grid=(bsz // SCAN_BATCH, seq // t_steps),
        in_specs=[blk, _const_spec(wb.shape), _const_spec(a_re.shape), _const_spec(a_im.shape),
                  _const_spec(wc.shape), _const_spec(d_skip.shape), _const_spec(w_glu.shape),
                  _const_spec(b_glu.shape)],
        out_specs=blk,
        out_shape=jax.ShapeDtypeStruct((bsz, seq, width), BF16),
        scratch_shapes=[pltpu.VMEM((rows, width), F32),
                        pltpu.VMEM((nslab, rows, LANES), F32),
                        pltpu.VMEM((nslab, SCAN_BATCH, LANES), F32),
                        pltpu.VMEM((rows, width), F32)],
        compiler_params=_cparams(("parallel", "arbitrary")),
        name="s5",
    )(u, wb, a_re, a_im, wc, d_skip, w_glu, b_glu)


def _s5_params(lam_re, lam_im, log_step, b_re, b_im, c_re, c_im):
    step = jnp.exp(log_step.astype(F32))[:, None]
    lr = jnp.minimum(lam_re.astype(F32), -1e-4)
    li = lam_im.astype(F32)
    mag = jnp.exp(lr * step)
    ang = li * step
    ab_re, ab_im = mag * jnp.cos(ang), mag * jnp.sin(ang)
    den = lr * lr + li * li
    nr, ni = ab_re - 1.0, ab_im
    f_re = (nr * lr + ni * li) / den
    f_im = (ni * lr - nr * li) / den
    bb_re = f_re[:, :, None] * b_re - f_im[:, :, None] * b_im
    bb_im = f_re[:, :, None] * b_im + f_im[:, :, None] * b_re
    g, n, c = bb_re.shape
    eye2 = jnp.eye(2, dtype=F32)
    eye_b = jnp.eye(S5_PAIRS_PER_BLOCK, dtype=F32)

    def in_mat(bb):
        bt = bb.transpose(0, 2, 1).reshape(S5_PAIRS, 2, c, n)
        m = jnp.einsum('pgcn,gh->pgchn', bt, eye2).reshape(S5_PAIRS, 2 * c, 2 * n)
        slot = jnp.arange(S5_PAIRS) % S5_PAIRS_PER_BLOCK
        sel = eye_b[slot]
        return jnp.einsum('pkn,ps->pskn', m, sel).reshape(S5_PAIRS, LANES, 2 * n)

    wb = jnp.concatenate([in_mat(bb_re), in_mat(bb_im)], axis=-1).astype(BF16)

    pairs_per_out = S5_PAIRS // (HALF_MIX // LANES)
    eye_o = jnp.eye(pairs_per_out, dtype=F32)

    def out_mat(cc):
        ct = cc.transpose(0, 2, 1).reshape(S5_PAIRS, 2, n, c)
        m = jnp.einsum('pgnc,gh->pgnhc', ct, eye2).reshape(S5_PAIRS, 2 * n, 2 * c)
        slot = jnp.arange(S5_PAIRS) % pairs_per_out
        sel = eye_o[slot]
        return jnp.einsum('pnk,ps->pnsk', m, sel).reshape(S5_PAIRS, 2 * n, LANES)

    wc = jnp.concatenate([out_mat(c_re.astype(F32)), -out_mat(c_im.astype(F32))],
                         axis=1).astype(BF16)
    a_re = ab_re.reshape(S5_PAIRS, 2 * n)
    a_im = ab_im.reshape(S5_PAIRS, 2 * n)
    return wb, a_re, a_im, wc


def _retention_kernel(q_ref, k_ref, v_ref, g_ref, ng_ref, din_ref, xi_ref, zeta_ref, mask_ref,
                      cd_ref, o_ref):
    seq = q_ref.shape[0]
    c = RET_CHUNK
    nc = seq // c
    heads = LANES // RET_QK
    r = [None] * heads
    for ci in range(nc):
        rows = slice(ci * c, (ci + 1) * c)
        qb = q_ref[rows, :]
        kf = k_ref[rows, :].astype(F32)
        km = jnp.concatenate([(kf * mask_ref[hh]).astype(BF16) for hh in range(heads)], axis=0)
        inner_all = lax.dot_general(qb, km, (((1,), (1,)), ((), ())), preferred_element_type=F32)
        if ci > 0:
            cross_all = jnp.dot(qb, jnp.concatenate(r, axis=1).astype(BF16),
                                preferred_element_type=F32)
        for hh in range(heads):
            lanes = slice(hh * RET_V, (hh + 1) * RET_V)
            vb = v_ref[rows, lanes]
            inner = inner_all[:, hh * c:(hh + 1) * c] * din_ref[hh]
            o = jnp.dot(inner.astype(BF16), vb, preferred_element_type=F32)
            if ci > 0:
                o = o + cross_all[:, lanes] * xi_ref[hh]
            if ci + 1 < nc:
                kz = (kf * zeta_ref[hh]).astype(BF16)
                upd = lax.dot_general(kz, vb, (((0,), (0,)), ((), ())),
                                      preferred_element_type=F32)
                r[hh] = upd if r[hh] is None else cd_ref[hh] * r[hh] + upd
            mu = jnp.mean(o, axis=-1, keepdims=True)
            oc = o - mu
            var = jnp.mean(oc * oc, axis=-1, keepdims=True)
            on = oc * lax.rsqrt(var + LN_EPS) * ng_ref[...]
            gt = g_ref[rows, lanes]
            o_ref[rows, lanes] = (gt * _sigmoid(gt) * on).astype(o_ref.dtype)


def retention(q, k, v, g, norm_g, tables):
    bsz, seq, _ = q.shape
    width = v.shape[2]
    heads = LANES // RET_QK
    pairs = width // (heads * RET_V)
    qk_spec = pl.BlockSpec((None, seq, LANES), lambda b, p: (b, 0, p))
    v_spec = pl.BlockSpec((None, seq, heads * RET_V), lambda b, p: (b, 0, p))
    tables = [t.reshape((pairs, heads) + t.shape[1:]) for t in tables]
    tab_specs = [pl.BlockSpec((None,) + t.shape[1:], lambda b, p: (p, 0, 0, 0)) for t in tables]
    return pl.pallas_call(
        _retention_kernel,
        grid=(bsz, pairs),
        in_specs=[qk_spec, qk_spec, v_spec, v_spec, _const_spec(norm_g.shape)] + tab_specs,
        out_specs=v_spec,
        out_shape=jax.ShapeDtypeStruct((bsz, seq, width), BF16),
        compiler_params=_cparams(("parallel", "parallel")),
        name="retention",
    )(q, k, v, g, norm_g, *tables)


def _retention_tables():
    c = RET_CHUNK
    log_g = jnp.log(1.0 - jnp.exp2(-5.0 - jnp.arange(RET_HEADS, dtype=F32)))
    idx = jnp.arange(c, dtype=F32)
    diff = idx[:, None] - idx[None, :]
    causal = diff >= 0
    decay_in = jnp.where(causal[None], jnp.exp(log_g[:, None, None] * jnp.where(causal, diff, 0.0)[None]), 0.0)
    xi = jnp.exp(log_g[:, None] * (idx + 1.0))
    zeta = jnp.exp(log_g[:, None] * (c - 1.0 - idx))
    chunk_decay = jnp.exp(log_g * c)
    bc = lambda t: jnp.broadcast_to(t[:, :, None], (RET_HEADS, c, LANES))
    cd = jnp.broadcast_to(chunk_decay[:, None, None], (RET_HEADS, 1, LANES))
    lane_head = jnp.arange(LANES) // RET_QK
    mask = (lane_head[None, :] == (jnp.arange(RET_HEADS) % (LANES // RET_QK))[:, None]).astype(F32)
    mask = mask[:, None, :]
    return decay_in, bc(xi), bc(zeta) * mask, mask, cd


def _kv_proj_kernel(m_ref, w_ref, k_ref, v_ref):
    mb = m_ref[...].astype(BF16)
    d = k_ref.shape[1]
    k_ref[...] = jnp.dot(mb, w_ref[:, :d], preferred_element_type=F32).astype(k_ref.dtype)
    v_ref[...] = jnp.dot(mb, w_ref[:, d:], preferred_element_type=F32).astype(v_ref.dtype)


def kv_proj(mem, w_all):
    n, d = mem.shape
    depth = w_all.shape[0]
    out_spec = pl.BlockSpec((None, TM, d), lambda l, i: (l, i, 0))
    return pl.pallas_call(
        _kv_proj_kernel,
        grid=(depth, n // TM),
        in_specs=[pl.BlockSpec((TM, d), lambda l, i: (i, 0)),
                  pl.BlockSpec((None,) + w_all.shape[1:], lambda l, i: (l, 0, 0))],
        out_specs=[out_spec, out_spec],
        out_shape=[jax.ShapeDtypeStruct((depth, n, d), BF16)] * 2,
        compiler_params=_cparams(("parallel", "parallel")),
        name="kv_proj",
    )(mem, w_all)


def _mix_out_xattn_kernel(x_ref, a_ref, b_ref, wm_ref, k_ref, v_ref, wq_ref, wo_ref,
                          g1_ref, b1_ref, g2_ref, b2_ref, o_ref, *scratch):
    half = a_ref.shape[1]
    hrows = x_ref.shape[0] // 2

    def stages(r):
        rows = slice(r * hrows, (r + 1) * hrows)
        y1_ref, x1_ref, y2_ref, q_ref, oh_ref = scratch[5 * r:5 * r + 5]

        def mix_out():
            m = jnp.dot(a_ref[rows, :], wm_ref[:half, :], preferred_element_type=F32)
            m = m + jnp.dot(b_ref[rows, :], wm_ref[half:, :], preferred_element_type=F32)
            y1_ref[...] = DN_ALPHA * x_ref[rows, :] + m

        def norm1():
            x1_ref[...] = _layer_norm(y1_ref[...], g1_ref[...], b1_ref[...])

        def q_proj():
            q = jnp.dot(x1_ref[...].astype(BF16), wq_ref[...], preferred_element_type=F32)
            q_ref[...] = (q * (XA_HEAD ** -0.5)).astype(BF16)

        def attend():
            for h in range(XA_HEADS):
                cols = slice(h * XA_HEAD, (h + 1) * XA_HEAD)
                s = lax.dot_general(q_ref[:, cols], k_ref[:, cols], (((1,), (1,)), ((), ())),
                                    preferred_element_type=F32)
                s = s - jnp.max(s, axis=-1, keepdims=True)
                p = jnp.exp(s)
                p = p / jnp.sum(p, axis=-1, keepdims=True)
                oh_ref[:, cols] = jnp.dot(p.astype(BF16), v_ref[:, cols],
                                          preferred_element_type=F32).astype(BF16)

        def o_proj():
            c = jnp.dot(oh_ref[...], wo_ref[...], preferred_element_type=F32)
            y2_ref[...] = DN_ALPHA * x1_ref[...] + c

        def norm2():
            o_ref[rows, :] = _layer_norm(y2_ref[...], g2_ref[...], b2_ref[...])

        return [mix_out, norm1, q_proj, attend, o_proj, norm2]

    _emit_skewed(stages(0), stages(1))


def mix_out_xattn(x, a, b, w_mix, k, v, wq_all, wo_all, layer, g1, b1, g2, b2, seq):
    n, d = x.shape
    half = a.shape[1]
    tm = min(TM, seq)
    tiles_per_seq = seq // tm
    row_spec = lambda w: pl.BlockSpec((tm, w), lambda i: (i, 0))
    kv_spec = pl.BlockSpec((None, MEM_LEN, d), lambda i: (layer, i // tiles_per_seq, 0))
    vec = _const_spec((1, d))
    return pl.pallas_call(
        _mix_out_xattn_kernel,
        grid=(n // tm,),
        in_specs=[row_spec(d), row_spec(half), row_spec(half), _const_spec(w_mix.shape),
                  kv_spec, kv_spec, _stacked_spec(wq_all, (layer,)), _stacked_spec(wo_all, (layer,)),
                  vec, vec, vec, vec],
        out_specs=row_spec(d),
        out_shape=jax.ShapeDtypeStruct((n, d), F32),
        scratch_shapes=[pltpu.VMEM((tm // 2, d), dt)
                        for _ in range(2) for dt in (F32, F32, F32, BF16, BF16)],
        compiler_params=_cparams(("parallel",)),
        name="mix_out_xattn",
    )(x, a, b, w_mix, k, v, wq_all, wo_all, g1, b1, g2, b2)


def _lru_gate_matrix(gate_w):
    eye = jnp.eye(LRU_BLOCKS, dtype=F32)
    dense = jnp.einsum('gncd,nm->gncmd', gate_w.astype(F32), eye)
    width = LRU_BLOCKS * LRU_BLOCK
    dense = dense.reshape(2, width, width)
    return jnp.concatenate([dense[0], dense[1]], axis=-1).astype(BF16)


def _diff_lambda_init(layer):
    return 0.8 - 0.6 * math.exp(-0.3 * layer)


EVEN_SECTIONS = (
    (0, HALF_MIX, True, DA_HEAD ** -0.5 * math.log2(math.e)),
    (HALF_MIX, HALF_MIX, True, 1.0),
    (2 * HALF_MIX, HALF_MIX, False, 1.0),
    (3 * HALF_MIX, HALF_MIX, False, 1.0),
    (4 * HALF_MIX, HALF_MIX, False, 1.0),
)
EVEN_DTYPES = (BF16, BF16, BF16, F32, F32)
RET_QK_W = RET_HEADS * RET_QK
ODD_SECTIONS = (
    (0, HALF_MIX, False, 1.0),
    (HALF_MIX, RET_QK_W, True, 1.0),
    (HALF_MIX + RET_QK_W, RET_QK_W, True, RET_QK ** -0.5),
    (HALF_MIX + 2 * RET_QK_W, HALF_MIX, False, 1.0),
    (2 * HALF_MIX + 2 * RET_QK_W, HALF_MIX, False, 1.0),
)
ODD_DTYPES = (F32, BF16, BF16, BF16, F32)


def kernel(x, mem, ln_g, ln_b, ffn_w_gate, ffn_w_up, ffn_w_down, xa_w_q, xa_w_kv, xa_w_o,
           ev_w_in, ev_w_out, da_lambda, da_norm_g, lru_conv_w, lru_conv_b, lru_gate_w,
           lru_gate_b, lru_lambda, od_w_in, od_w_out, s5_lam_re, s5_lam_im, s5_log_step,
           s5_b_re, s5_b_im, s5_c_re, s5_c_im, s5_d, s5_glu_w, s5_glu_b, ret_norm_g):
    bsz, seq, d = x.shape
    n = bsz * seq
    depth = ln_g.shape[0]
    mem2 = mem.reshape(bsz * mem.shape[1], d)
    da_tables = _rotary_tables(seq, DA_ROPE, ROPE_THETA, DA_HEAD)
    ret_rot_tables = _rotary_tables(seq, RET_QK, RET_THETA, RET_QK)
    ret_tables = _retention_tables()
    row = lambda v: v.reshape(1, -1).astype(F32)

    wg, wu, wd = ffn_w_gate.astype(BF16), ffn_w_up.astype(BF16), ffn_w_down.astype(BF16)
    wq_all, wo_all = xa_w_q.astype(BF16), xa_w_o.astype(BF16)
    k_all, v_all = kv_proj(mem2, xa_w_kv.astype(BF16))

    h = x.reshape(n, d)
    for l in range(depth):
        h = ffn_ln(h, wg, wu, wd, (l, 0), row(ln_g[l, 0]), row(ln_b[l, 0]))

        if l % 2 == 0:
            e = l // 2
            q, k, v, gate, xr = in_proj(h, ev_w_in[e].astype(BF16), da_tables, EVEN_SECTIONS,
                                        EVEN_DTYPES, DA_ROPE // 2, seq)
            shp = (bsz, seq, HALF_MIX)
            a_out = diff_attn(q.reshape(shp), k.reshape(shp), v.reshape(shp), da_lambda[e],
                              row(da_norm_g[e]), _diff_lambda_init(l))
            b_out = rglru(xr.reshape(shp), gate.reshape(shp), lru_conv_w[e], row(lru_conv_b[e]),
                          _lru_gate_matrix(lru_gate_w[e]), row(lru_gate_b[e]), row(lru_lambda[e]))
            w_out = ev_w_out[e]
        else:
            o = l // 2
            u, q, k, v, g = in_proj(h, od_w_in[o].astype(BF16), ret_rot_tables, ODD_SECTIONS,
                                    ODD_DTYPES, RET_QK // 2, seq)
            shp = (bsz, seq, HALF_MIX)
            wb, a_re, a_im, wc = _s5_params(s5_lam_re[o], s5_lam_im[o], s5_log_step[o],
                                            s5_b_re[o], s5_b_im[o], s5_c_re[o], s5_c_im[o])
            a_out = s5(u.reshape(shp), wb, a_re, a_im, wc, row(s5_d[o]),
                       s5_glu_w[o].astype(BF16), row(s5_glu_b[o]))
            qk_shp = (bsz, seq, RET_QK_W)
            b_out = retention(q.reshape(qk_shp), k.reshape(qk_shp), v.reshape(shp),
                              g.reshape(shp), row(ret_norm_g[o]), ret_tables)
            w_out = od_w_out[o]
        h = mix_out_xattn(h, a_out.reshape(n, HALF_MIX), b_out.reshape(n, HALF_MIX),
                          w_out.astype(BF16), k_all, v_all, wq_all, wo_all, l,
                          row(ln_g[l, 1]), row(ln_b[l, 1]), row(ln_g[l, 2]), row(ln_b[l, 2]), seq)
        h = ffn_ln(h, wg, wu, wd, (l, 1), row(ln_g[l, 3]), row(ln_b[l, 3]))
    return h.reshape(bsz, seq, d)
```
